```python
import math
import jax, jax.numpy as jnp
from jax import lax
import numpy as np

D_MODEL = 4096
BATCH = 4
SEQ = 4096
DEPTH = 4
DEC_BATCH = 16
DEC_SEQ = 16
PAST_LEN = 1024

F32 = jnp.float32
CHUNK = 64
N_EVEN = (DEPTH + 1) // 2
N_ODD = DEPTH // 2
EPS = 1e-6
D_A = D_MODEL // 2
G_A = 8
DG_A = D_A // G_A
MLP_CHUNK = 128
D_B = D_MODEL // 2
DH_B = 64
H_B = D_B // DH_B
LORA_W = 64
LORA_A = 64
LORA_G = 256
RWKV_GN_EPS = 64e-5
P_B = 3 * D_B + LORA_W + LORA_A + LORA_G
RWKV_SPLITS = (D_B, D_B, D_B, LORA_W, LORA_A, LORA_G)
P_EVEN = 2 * D_A + P_B
DK_C = 128
DV_C = 256
H_C = D_MODEL // (2 * DV_C)
ROPE_BASE = 10000.0
DH_D = 128
H_D = D_MODEL // (2 * DH_D)
Q_BLOCK = 128
FOX_BIAS_INIT = 3.0
ODD_SPLITS = (H_C * DK_C, H_C * DK_C, H_C * DV_C, H_C * DV_C, H_D * DH_D, H_D * DH_D, H_D * DH_D, H_D)
P_ODD = 2 * H_C * DK_C + 2 * H_C * DV_C + 3 * H_D * DH_D + H_D
H_P = 8
N_KEYS = 128
N_EXP = N_KEYS * N_KEYS
DQ_P = 256
DQ_HALF = DQ_P // 2
TOPK_P = 16
PEER_BLOCK = 128

kernel_name = 'streaming_hybrid_gmlp_rwkv7_retnet_fox_peer'


def split_cols(z, sizes):
    return jnp.split(z, [int(s) for s in np.cumsum(sizes)[:-1]], axis=-1)


def rmsnorm(x, g):
    xf = x.astype(F32)
    return (xf * lax.rsqrt(jnp.mean(xf * xf, axis=-1, keepdims=True) + EPS)).astype(x.dtype) * g


def group_norm(y, eps):
    y = y.astype(F32)
    mean = jnp.mean(y, -1, keepdims=True)
    var = jnp.mean(jnp.square(y - mean), -1, keepdims=True)
    return (y - mean) * lax.rsqrt(var + eps)


def rope(x, pos):
    half = x.shape[-1] // 2
    inv = ROPE_BASE ** (-jnp.arange(half, dtype=F32) / half)
    ang = pos.astype(F32)[:, None] * inv[None, :]
    cos, sin = jnp.cos(ang)[:, None, :], jnp.sin(ang)[:, None, :]
    x1, x2 = x[..., :half].astype(F32), x[..., half:].astype(F32)
    return jnp.concatenate([x1 * cos - x2 * sin, x1 * sin + x2 * cos], -1).astype(x.dtype)


def spatial_gating(pa, norm_g, w_s, b_s):
    B, T, _ = pa.shape
    z = jax.nn.gelu(pa)
    u, v = z[..., :D_A], z[..., D_A:]
    vn = group_norm(v.reshape(B, T, G_A, DG_A), EPS).reshape(B, T, D_A).astype(pa.dtype) * norm_g
    pad = (-T) % MLP_CHUNK
    vc = jnp.pad(vn, ((0, 0), (0, pad), (0, 0))).reshape(B, -1, MLP_CHUNK, G_A, DG_A)
    w = jnp.tril(w_s)
    s = jnp.einsum('gij,bnjgd->bnigd', w, vc) + b_s.T[:, :, None]
    s = s.reshape(B, -1, D_A)[:, :T]
    return u * s, vn


def rwkv7_scan(S0, r, w, k, v, kk, a):
    def step(S, inp):
        r_t, w_t, k_t, v_t, kk_t, a_t = inp
        sa = jnp.einsum('bhvk,bhk->bhv', S, -kk_t)
        S = S * w_t[:, :, None, :] + sa[..., None] * (kk_t * a_t)[:, :, None, :] + v_t[..., None] * k_t[:, :, None, :]
        return S, jnp.einsum('bhvk,bhk->bhv', S, r_t)
    xs = tuple(jnp.moveaxis(z.astype(F32), 1, 0) for z in (r, w, k, v, kk, a))
    S, ys = lax.scan(step, S0.astype(F32), xs)
    return jnp.moveaxis(ys, 0, 1), S


def rwkv7_time_mix(pb, prev_row, S0, mu, w0, w_up, a0, a_up, g_up, k_k, k_a, r_k, ln_w, ln_b):
    B, T, _ = pb.shape
    prev = jnp.concatenate([prev_row.astype(pb.dtype), pb[:, :-1]], axis=1)
    xs = pb + (prev - pb) * mu
    r, k, v, wd, ad, gd = split_cols(xs, RWKV_SPLITS)
    w_log = -jax.nn.softplus(-(w0 + jnp.tanh(wd) @ w_up).astype(F32)) - 0.5
    decay = jnp.exp(-jnp.exp(w_log))
    a = jax.nn.sigmoid(a0 + ad @ a_up)
    g = jax.nn.sigmoid(gd) @ g_up
    heads = lambda z: z.reshape(B, T, H_B, DH_B)
    kk = heads(k * k_k).astype(F32)
    kk = kk * lax.rsqrt(jnp.maximum(jnp.sum(kk * kk, -1, keepdims=True), 1e-24))
    k = k * (1 + (a - 1) * k_a)
    rh, kh, vh = heads(r), heads(k), heads(v)
    y, S = rwkv7_scan(S0, rh, heads(decay), kh, vh, kk, heads(a))
    y = group_norm(y, RWKV_GN_EPS).reshape(B, T, D_B) * ln_w + ln_b
    bonus = jnp.sum(rh * kh * r_k, -1, keepdims=True) * vh
    out = (y + bonus.reshape(B, T, D_B)) * g
    return out.astype(pb.dtype), pb[:, -1:], S


def even_mixer(h, prev_row, S0, w_in, w_out, sgu_g, sgu_w, sgu_b, mu, w0, w_up, a0, a_up, g_up, k_k, k_a, r_k, ln_w, ln_b):
    p = h @ w_in
    out_a, vn = spatial_gating(p[..., :2 * D_A], sgu_g, sgu_w, sgu_b)
    out_b, row, S = rwkv7_time_mix(p[..., 2 * D_A:], prev_row, S0, mu, w0, w_up, a0, a_up, g_up, k_k, k_a, r_k, ln_w, ln_b)
    y = jnp.concatenate([out_a, out_b.astype(out_a.dtype)], axis=-1) @ w_out
    return y.astype(h.dtype), vn, row, S


def retention(q, k, v, R0):
    B, T, H, dk = q.shape
    dv = v.shape[-1]
    L = math.gcd(T, CHUNK)
    n = T // L
    log_g = jnp.log1p(-jnp.exp2(-5.0 - jnp.arange(H, dtype=F32)))
    idx = jnp.arange(L, dtype=F32)
    rel = idx[:, None] - idx[None, :]
    dmat = jnp.where(rel >= 0, jnp.exp(jnp.maximum(rel, 0.0) * log_g[:, None, None]), 0.0)
    qc = q.astype(F32).reshape(B, n, L, H, dk)
    kc = k.astype(F32).reshape(B, n, L, H, dk)
    vc = v.astype(F32).reshape(B, n, L, H, dv)
    att = jnp.einsum('bnihd,bnjhd->bnhij', qc, kc) * dmat
    o_in = jnp.einsum('bnhij,bnjhe->bnihe', att, vc)
    k_dec = jnp.exp((L - 1 - idx)[:, None] * log_g[None, :])
    kv = jnp.einsum('bnjhd,bnjhe->nbhde', kc * k_dec[:, :, None], vc)
    g_L = jnp.exp(L * log_g)[None, :, None, None]

    def step(R, kv_i):
        return R * g_L + kv_i, R
    R_fin, R_prev = lax.scan(step, R0.astype(F32), kv)
    q_dec = jnp.exp((idx + 1)[:, None] * log_g[None, :])
    o_x = jnp.einsum('bnihd,nbhde->bnihe', qc * q_dec[:, :, None], R_prev)
    return (o_in + o_x).reshape(B, T, H, dv), R_fin


def fox_prompt(q, k, v, logf):
    B, T, H, d = q.shape
    nb = T // Q_BLOCK
    Fh = jnp.cumsum(logf, axis=1).transpose(0, 2, 1)
    qb = jnp.moveaxis(q.reshape(B, nb, Q_BLOCK, H, d), 1, 0)
    Fq = jnp.moveaxis(Fh.reshape(B, H, nb, Q_BLOCK), 2, 0)
    kpos = jnp.arange(T)
    scale = d ** -0.5

    def block(args):
        qi, Fi, bi = args
        s = jnp.einsum('bqhd,bkhd->bhqk', qi, k).astype(F32) * scale + (Fi[..., None] - Fh[:, :, None, :])
        qpos = bi * Q_BLOCK + jnp.arange(Q_BLOCK)
        s = jnp.where(kpos[None, :] <= qpos[:, None], s, -jnp.inf)
        p = jax.nn.softmax(s, axis=-1).astype(v.dtype)
        return jnp.einsum('bhqk,bkhd->bqhd', p, v)
    o = lax.map(block, (qb, Fq, jnp.arange(nb)))
    return jnp.moveaxis(o, 0, 1).reshape(B, T, H, d)


def fox_cached(q, k, v, logf, k_past, v_past, logf_past):
    P, T = k_past.shape[1], q.shape[1]
    kk = jnp.concatenate([k_past.astype(k.dtype), k], axis=1)
    vv = jnp.concatenate([v_past.astype(v.dtype), v], axis=1)
    Fh = jnp.cumsum(jnp.concatenate([logf_past.astype(F32), logf], axis=1), axis=1).transpose(0, 2, 1)
    s = jnp.einsum('bqhd,bkhd->bhqk', q, kk).astype(F32) * (q.shape[-1] ** -0.5)
    s = s + (Fh[:, :, P:, None] - Fh[:, :, None, :])
    mask = jnp.arange(P + T)[None, :] <= (P + jnp.arange(T))[:, None]
    p = jax.nn.softmax(jnp.where(mask, s, -jnp.inf), axis=-1).astype(vv.dtype)
    return jnp.einsum('bhqk,bkhd->bqhd', p, vv)


def odd_mixer(h, pos, R0, fox_past, w_in, w_out, gn_g, b_f):
    B, T, _ = h.shape
    p = h @ w_in
    cq, ck, cv, cg, dq, dk, dv, df = split_cols(p, ODD_SPLITS)
    q = rope(cq.reshape(B, T, H_C, DK_C), pos)
    k = rope(ck.reshape(B, T, H_C, DK_C), pos) * (DK_C ** -0.5)
    o, R = retention(q, k, cv.reshape(B, T, H_C, DV_C), R0)
    out_c = jax.nn.silu(cg) * (group_norm(o, EPS).reshape(B, T, H_C * DV_C) * gn_g).astype(h.dtype)
    logf = jax.nn.log_sigmoid((df + b_f).astype(F32))
    qd, kd, vd = (z.reshape(B, T, H_D, DH_D) for z in (dq, dk, dv))
    if fox_past is None:
        od = fox_prompt(qd, kd, vd, logf)
    else:
        od = fox_cached(qd, kd, vd, logf, fox_past[0], fox_past[1], fox_past[2])
    y = jnp.concatenate([out_c, od.reshape(B, T, H_D * DH_D).astype(out_c.dtype)], axis=-1) @ w_out
    return y.astype(h.dtype), R, kd, vd, logf


def peer_ffn(h, wq, subkeys, u_tab, v_tab):
    B, T, D = h.shape
    xt = h.reshape(B * T, D)
    n = xt.shape[0]
    pad = (-n) % PEER_BLOCK
    xb = jnp.pad(xt, ((0, pad), (0, 0))).reshape(-1, PEER_BLOCK, D)

    def block(xi):
        q = (xi @ wq).reshape(PEER_BLOCK, H_P, 2, DQ_HALF)
        s = jnp.einsum('thcd,hcnd->thcn', q, subkeys).astype(F32)
        v1, i1 = lax.top_k(s[:, :, 0], TOPK_P)
        v2, i2 = lax.top_k(s[:, :, 1], TOPK_P)
        cand = (v1[..., :, None] + v2[..., None, :]).reshape(PEER_BLOCK, H_P, TOPK_P * TOPK_P)
        sc, ci = lax.top_k(cand, TOPK_P)
        e = (jnp.take_along_axis(i1, ci // TOPK_P, axis=-1) * N_KEYS
             + jnp.take_along_axis(i2, ci % TOPK_P, axis=-1)).reshape(PEER_BLOCK, H_P * TOPK_P)
        gate = jax.nn.softmax(sc, axis=-1).reshape(PEER_BLOCK, H_P * TOPK_P).astype(xi.dtype)
        u = jnp.take(u_tab, e, axis=0)
        act = jax.nn.gelu(jnp.einsum('td,tkd->tk', xi, u)) * gate
        return jnp.einsum('tk,tkd->td', act, jnp.take(v_tab, e, axis=0))
    y = lax.map(block, xb).reshape(-1, D)[:n]
    return y.reshape(B, T, D)


def setup_inputs(seed: int = 0) -> dict:
    key = jax.random.key(seed)
    keys = iter(jax.random.split(key, 48))

    def nrm(shape, scale=1.0, offset=0.0):
        return offset + scale * jax.random.normal(next(keys), shape, F32)

    def unif(shape):
        return jax.random.uniform(next(keys), shape, F32)
    d = D_MODEL
    return {
        'x_prompt': nrm((BATCH, SEQ, d)),
        'x_sample': nrm((DEC_BATCH, DEC_SEQ, d)),
        'c_prompt': nrm((BATCH, d)),
        'c_sample': nrm((DEC_BATCH, d)),
        'state_rwkv_shift': nrm((N_EVEN, DEC_BATCH, 1, P_B)),
        'state_rwkv': nrm((N_EVEN, DEC_BATCH, H_B, DH_B, DH_B), 0.5),
        'state_ret': nrm((N_ODD, DEC_BATCH, H_C, DK_C, DV_C), 0.5),
        'cache_fox_k': nrm((N_ODD, DEC_BATCH, PAST_LEN, H_D, DH_D)),
        'cache_fox_v': nrm((N_ODD, DEC_BATCH, PAST_LEN, H_D, DH_D)),
        'cache_fox_logf': jax.nn.log_sigmoid(nrm((N_ODD, DEC_BATCH, PAST_LEN, H_D), 1.0, FOX_BIAS_INIT)),
        'ada_w': nrm((DEPTH, d, 6 * d), 0.5 * d ** -0.5),
        'ada_b': nrm((DEPTH, 6 * d), 0.01),
        'norm_g': nrm((DEPTH, 2, d), 0.1, 1.0),
        'final_g': nrm((d,), 0.1, 1.0),
        'w_in_even': nrm((N_EVEN, d, P_EVEN), d ** -0.5),
        'w_out_even': nrm((N_EVEN, D_A + D_B, d), (D_A + D_B) ** -0.5),
        'sgu_norm_g': nrm((N_EVEN, D_A), 0.1, 1.0),
        'sgu_w': nrm((N_EVEN, G_A, MLP_CHUNK, MLP_CHUNK), MLP_CHUNK ** -0.5),
        'sgu_b': nrm((N_EVEN, G_A, MLP_CHUNK), 0.1, 1.0),
        'rwkv_mu': unif((N_EVEN, P_B)),
        'rwkv_w0': nrm((N_EVEN, D_B), 0.5),
        'rwkv_w_up': nrm((N_EVEN, LORA_W, D_B), LORA_W ** -0.5),
        'rwkv_a0': nrm((N_EVEN, D_B), 0.1),
        'rwkv_a_up': nrm((N_EVEN, LORA_A, D_B), LORA_A ** -0.5),
        'rwkv_g_up': nrm((N_EVEN, LORA_G, D_B), LORA_G ** -0.5),
        'rwkv_k_k': nrm((N_EVEN, D_B), 0.1, 0.85),
        'rwkv_k_a': nrm((N_EVEN, D_B), 0.1, 1.0),
        'rwkv_r_k': nrm((N_EVEN, H_B, DH_B), 0.1),
        'rwkv_ln_w': nrm((N_EVEN, D_B), 0.1, 1.0),
        'rwkv_ln_b': nrm((N_EVEN, D_B), 0.01),
        'w_in_odd': nrm((N_ODD, d, P_ODD), d ** -0.5),
        'w_out_odd': nrm((N_ODD, H_C * DV_C + H_D * DH_D, d), (H_C * DV_C + H_D * DH_D) ** -0.5),
        'ret_gn_g': nrm((N_ODD, H_C * DV_C), 0.1, 1.0),
        'fox_b_f': nrm((N_ODD, H_D), 0.1, FOX_BIAS_INIT),
        'peer_wq': nrm((DEPTH, d, H_P * DQ_P), d ** -0.5),
        'peer_subkeys': nrm((DEPTH, H_P, 2, N_KEYS, DQ_HALF), DQ_HALF ** -0.5),
        'peer_u': nrm((DEPTH, N_EXP, d), d ** -0.5),
        'peer_v': nrm((DEPTH, N_EXP, d), H_P ** -0.5),
    }


def reference(x_prompt, x_sample, c_prompt, c_sample, state_rwkv_shift, state_rwkv, state_ret,
              cache_fox_k, cache_fox_v, cache_fox_logf, ada_w, ada_b, norm_g, final_g,
              w_in_even, w_out_even, sgu_norm_g, sgu_w, sgu_b, rwkv_mu, rwkv_w0, rwkv_w_up,
              rwkv_a0, rwkv_a_up, rwkv_g_up, rwkv_k_k, rwkv_k_a, rwkv_r_k, rwkv_ln_w, rwkv_ln_b,
              w_in_odd, w_out_odd, ret_gn_g, fox_b_f, peer_wq, peer_subkeys, peer_u, peer_v):

    def run(x, c, pos0, shift0, rwkv0, ret0, fox_past):
        pos = pos0 + jnp.arange(x.shape[1])
        v_rows, shifts, rwkvs, rets, ks, vs, lfs = [], [], [], [], [], [], []
        for i in range(DEPTH):
            j = i // 2
            mod = (jax.nn.silu(c) @ ada_w[i] + ada_b[i])[:, None, :]
            sh1, sc1, g1, sh2, sc2, g2 = jnp.split(mod, 6, axis=-1)
            h = rmsnorm(x, norm_g[i, 0]) * (1 + sc1) + sh1
            if i % 2 == 0:
                y, vn, row, S = even_mixer(h, shift0[j], rwkv0[j], w_in_even[j], w_out_even[j], sgu_norm_g[j],
                                           sgu_w[j], sgu_b[j], rwkv_mu[j], rwkv_w0[j], rwkv_w_up[j], rwkv_a0[j],
                                           rwkv_a_up[j], rwkv_g_up[j], rwkv_k_k[j], rwkv_k_a[j], rwkv_r_k[j],
                                           rwkv_ln_w[j], rwkv_ln_b[j])
                v_rows.append(vn)
                shifts.append(row)
                rwkvs.append(S)
            else:
                y, R, kd, vd, lf = odd_mixer(h, pos, ret0[j], fox_past[j], w_in_odd[j], w_out_odd[j],
                                             ret_gn_g[j], fox_b_f[j])
                rets.append(R)
                ks.append(kd)
                vs.append(vd)
                lfs.append(lf)
            x = x + g1 * y
            h = rmsnorm(x, norm_g[i, 1]) * (1 + sc2) + sh2
            x = x + g2 * peer_ffn(h, peer_wq[i], peer_subkeys[i], peer_u[i], peer_v[i])
        return rmsnorm(x, final_g), v_rows, shifts, rwkvs, rets, ks, vs, lfs

    bp = x_prompt.shape[0]
    y_p, _, sh_p, rw_p, rt_p, k_p, v_p, lf_p = run(
        x_prompt, c_prompt, 0,
        [jnp.zeros((bp, 1, P_B), x_prompt.dtype)] * N_EVEN,
        [jnp.zeros((bp, H_B, DH_B, DH_B), F32)] * N_EVEN,
        [jnp.zeros((bp, H_C, DK_C, DV_C), F32)] * N_ODD,
        [None] * N_ODD)
    y_s, vr_s, sh_s, rw_s, rt_s, k_s, v_s, lf_s = run(
        x_sample, c_sample, PAST_LEN,
        [state_rwkv_shift[j] for j in range(N_EVEN)],
        [state_rwkv[j] for j in range(N_EVEN)],
        [state_ret[j] for j in range(N_ODD)],
        [(cache_fox_k[j], cache_fox_v[j], cache_fox_logf[j]) for j in range(N_ODD)])
    return (y_p, y_s, jnp.stack(vr_s), jnp.stack(sh_p), jnp.stack(sh_s), jnp.stack(rw_p), jnp.stack(rw_s),
            jnp.stack(rt_p), jnp.stack(rt_s), jnp.stack(k_p), jnp.stack(k_s), jnp.stack(v_p), jnp.stack(v_s),
            jnp.stack(lf_p), jnp.stack(lf_s))
```

```python
import functools
import math

import jax
import jax.numpy as jnp
import numpy as np
from jax import lax
from jax.experimental import pallas as pl
from jax.experimental.pallas import tpu as pltpu

F32 = jnp.float32
BF16 = jnp.bfloat16

EPS = 1e-6
RWKV_GN_EPS = 64e-5
ROPE_BASE = 10000.0
RET_CHUNK = 64
MLP_CHUNK = 128
N_GROUPS_A = 8
DH_B = 64
DK_C = 128
DV_C = 256
DH_D = 128
TOPK_P = 16

LANES = 128
TAIL_BLOCK = 512
MIB = 1024 * 1024


def _round_up(n, m):
    return (n + m - 1) // m * m


def _params(sem, vmem_mib):
    return pltpu.CompilerParams(dimension_semantics=sem, vmem_limit_bytes=int(vmem_mib * MIB))


def _mm_kernel(a_ref, b_ref, o_ref):
    o_ref[...] = jnp.dot(a_ref[...], b_ref[...], preferred_element_type=F32).astype(o_ref.dtype)


def matmul(a, b, out_dtype=F32):
    M, K = a.shape
    N = b.shape[1]
    tm = min(M, 1024)
    tn = 512
    assert M % tm == 0 and N % tn == 0, (M, N)
    return pl.pallas_call(
        _mm_kernel,
        grid=(M // tm, N // tn),
        in_specs=[pl.BlockSpec((tm, K), lambda i, j: (i, 0)),
                  pl.BlockSpec((K, tn), lambda i, j: (0, j))],
        out_specs=pl.BlockSpec((tm, tn), lambda i, j: (i, j)),
        out_shape=jax.ShapeDtypeStruct((M, N), out_dtype),
        compiler_params=_params(("parallel", "parallel"), 48),
        name="matmul",
    )(a, b)


def _ada_kernel(c_ref, w_ref, b_ref, o_ref):
    c = c_ref[...]
    a = (c * jax.nn.sigmoid(c)).astype(BF16)
    o_ref[...] = jnp.dot(a, w_ref[...].astype(BF16), preferred_element_type=F32) + b_ref[...]


def ada_modulation(c_all, ada_w, ada_b):
    depth, d, n6 = ada_w.shape
    R = c_all.shape[0]
    tn = 512
    return pl.pallas_call(
        _ada_kernel,
        grid=(depth, n6 // tn),
        in_specs=[pl.BlockSpec((R, d), lambda i, j: (0, 0)),
                  pl.BlockSpec((None, d, tn), lambda i, j: (i, 0, j)),
                  pl.BlockSpec((None, 1, tn), lambda i, j: (i, 0, j))],
        out_specs=pl.BlockSpec((None, R, tn), lambda i, j: (i, 0, j)),
        out_shape=jax.ShapeDtypeStruct((depth, R, n6), F32),
        compiler_params=_params(("parallel", "parallel"), 40),
        name="ada_modulation",
    )(c_all, ada_w, ada_b.reshape(depth, 1, n6))


def _resnorm_kernel(*refs, has_res, y_transposed, has_mod):
    it = iter(refs)
    x_ref = next(it)
    if has_res:
        y_ref = next(it)
        gate_ref = next(it)
    g_ref = next(it)
    if has_mod:
        sc_ref = next(it)
        sh_ref = next(it)
    if has_res:
        xo_ref = next(it)
    h_ref = next(it)
    x = x_ref[...]
    if has_res:
        y = y_ref[...]
        if y_transposed:
            y = y.T
        x = x + gate_ref[...] * y
        xo_ref[...] = x
    xn = x * lax.rsqrt(jnp.mean(x * x, axis=-1, keepdims=True) + EPS) * g_ref[...]
    if has_mod:
        xn = xn * (1.0 + sc_ref[...]) + sh_ref[...]
    h_ref[...] = xn.astype(h_ref.dtype)


def resnorm(x, norm_g, *, res=None, mod=None, out_dtype=BF16):
    B, T, d = x.shape
    tt = min(T, 256)
    nt = T // tt
    row = pl.BlockSpec((None, tt, d), lambda b, i: (b, i, 0))
    args, specs = [x], [row]
    if res is not None:
        y, y_transposed, mod4, layer, row0, gk = res
        args.append(y)
        if y_transposed:
            specs.append(pl.BlockSpec((d, tt), lambda b, i: (0, b * nt + i)))
        else:
            specs.append(row)
        args.append(mod4)
        specs.append(pl.BlockSpec((None, None, 1, d), lambda b, i: (layer, row0 + b, 0, gk)))
    else:
        y_transposed = False
    args.append(norm_g.reshape(1, d))
    specs.append(pl.BlockSpec((1, d), lambda b, i: (0, 0)))
    if mod is not None:
        mod4m, layer_m, row0_m, sck, shk = mod
        args += [mod4m, mod4m]
        specs.append(pl.BlockSpec((None, None, 1, d), lambda b, i: (layer_m, row0_m + b, 0, sck)))
        specs.append(pl.BlockSpec((None, None, 1, d), lambda b, i: (layer_m, row0_m + b, 0, shk)))
    out_shape, out_specs = [], []
    if res is not None:
        out_shape.append(jax.ShapeDtypeStruct((B, T, d), F32))
        out_specs.append(row)
    out_shape.append(jax.ShapeDtypeStruct((B, T, d), out_dtype))
    out_specs.append(row)
    outs = pl.pallas_call(
        functools.partial(_resnorm_kernel, has_res=res is not None, y_transposed=y_transposed,
                          has_mod=mod is not None),
        grid=(B, nt),
        in_specs=specs,
        out_specs=out_specs,
        out_shape=out_shape,
        compiler_params=_params(("parallel", "parallel"), 48),
        name="resnorm",
    )(*args)
    if res is not None:
        return outs[0], outs[1]
    return None, outs[0]


def _sgu_kernel(pu_ref, pv_ref, ng_ref, w_ref, b_ref, oa_ref, vn_ref, *, groups):
    u = jax.nn.gelu(pu_ref[...])
    v = jax.nn.gelu(pv_ref[...])
    dg = v.shape[1] // groups
    for g in range(groups):
        sl = slice(g * dg, (g + 1) * dg)
        vg = v[:, sl]
        mean = jnp.mean(vg, axis=-1, keepdims=True)
        cen = vg - mean
        var = jnp.mean(cen * cen, axis=-1, keepdims=True)
        vn = cen * lax.rsqrt(var + EPS) * ng_ref[:, sl]
        vn_ref[:, sl] = vn
        s = jnp.dot(w_ref[g], vn.astype(BF16), preferred_element_type=F32) + b_ref[:, sl]
        oa_ref[:, sl] = (u[:, sl] * s).astype(oa_ref.dtype)


def spatial_gating(p, d_a, norm_g, w_s, b_s):
    B, T, _ = p.shape
    L = min(T, MLP_CHUNK)
    G = w_s.shape[0]
    dg = d_a // G
    w_tril = jnp.tril(w_s)[:, :L, :L].astype(BF16)
    b_full = jnp.repeat(b_s.T[:L], dg, axis=1)
    blk = lambda c: pl.BlockSpec((None, L, d_a), lambda b, i: (b, i, c))
    return pl.pallas_call(
        functools.partial(_sgu_kernel, groups=G),
        grid=(B, T // L),
        in_specs=[blk(0), blk(1),
                  pl.BlockSpec((1, d_a), lambda b, i: (0, 0)),
                  pl.BlockSpec((G, L, L), lambda b, i: (0, 0, 0)),
                  pl.BlockSpec((L, d_a), lambda b, i: (0, 0))],
        out_specs=[blk(0), blk(0)],
        out_shape=[jax.ShapeDtypeStruct((B, T, d_a), BF16), jax.ShapeDtypeStruct((B, T, d_a), F32)],
        compiler_params=_params(("parallel", "parallel"), 32),
        name="spatial_gating",
    )(p, p, norm_g.reshape(1, d_a), w_tril, b_full)


def _shifted(x_ref, halo_ref, state_ref, mu_ref):
    x = x_ref[...]
    first = jnp.where(pl.program_id(1) == 0, state_ref[...], halo_ref[7:8, :])
    rows = lax.broadcasted_iota(jnp.int32, x.shape, 0)
    prev = jnp.where(rows == 0, first, pltpu.roll(x, 1, axis=0))
    return x + (prev - x) * mu_ref[...]


def _rwkv_prep_kernel(r_ref, k_ref, v_ref, t_ref, rh_ref, kh_ref, vh_ref, th_ref,
                      sr_ref, sk_ref, sv_ref, st_ref, mr_ref, mk_ref, mv_ref, mt_ref,
                      w0_ref, wup_ref, a0_ref, aup_ref, gup_ref, kk_ref, ka_ref,
                      ro_ref, wo_ref, ko_ref, vo_ref, kko_ref, ao_ref, go_ref, *, lw, la, lg):
    r = _shifted(r_ref, rh_ref, sr_ref, mr_ref)
    k = _shifted(k_ref, kh_ref, sk_ref, mk_ref)
    v = _shifted(v_ref, vh_ref, sv_ref, mv_ref)
    t = _shifted(t_ref, th_ref, st_ref, mt_ref)
    wd, ad, gd = t[:, :lw], t[:, lw:lw + la], t[:, lw + la:lw + la + lg]
    wl = w0_ref[...] + jnp.dot(jnp.tanh(wd).astype(BF16), wup_ref[...], preferred_element_type=F32)
    w_log = -jax.nn.softplus(-wl) - 0.5
    wo_ref[...] = jnp.exp(-jnp.exp(w_log))
    a = jax.nn.sigmoid(a0_ref[...] + jnp.dot(ad.astype(BF16), aup_ref[...], preferred_element_type=F32))
    go_ref[...] = jnp.dot(jax.nn.sigmoid(gd).astype(BF16), gup_ref[...], preferred_element_type=F32)
    ro_ref[...] = r
    vo_ref[...] = v
    ao_ref[...] = a
    kko_ref[...] = k * kk_ref[...]
    ko_ref[...] = k * (1.0 + (a - 1.0) * ka_ref[...])


def rwkv_prep(p, d_b, col0, shift_state, mu, w0, w_up, a0, a_up, g_up, k_k, k_a):
    B, T, _ = p.shape
    lw, la, lg = w_up.shape[0], a_up.shape[0], g_up.shape[0]
    tt = min(T, 128)
    cb = col0 // d_b
    tb = (col0 + 3 * d_b) // TAIL_BLOCK
    assert col0 % d_b == 0 and (col0 + 3 * d_b) % TAIL_BLOCK == 0 and lw + la + lg <= TAIL_BLOCK
    pad = TAIL_BLOCK - (lw + la + lg)

    def seg(c, width):
        return pl.BlockSpec((None, tt, width), lambda b, i: (b, i, c))

    def halo(c, width):
        return pl.BlockSpec((None, 8, width), lambda b, i: (b, jnp.maximum(i * (tt // 8) - 1, 0), c))

    def per_b(width):
        return pl.BlockSpec((None, 1, width), lambda b, i: (b, 0, 0))

    def full(shape):
        return pl.BlockSpec(shape, lambda b, i: tuple(0 for _ in shape))

    st = shift_state
    s_parts = [st[..., :d_b], st[..., d_b:2 * d_b], st[..., 2 * d_b:3 * d_b],
               jnp.pad(st[..., 3 * d_b:], ((0, 0), (0, 0), (0, pad)))]
    m_parts = [mu[None, :d_b], mu[None, d_b:2 * d_b], mu[None, 2 * d_b:3 * d_b],
               jnp.pad(mu[None, 3 * d_b:], ((0, 0), (0, pad)))]
    out_row = pl.BlockSpec((None, tt, d_b), lambda b, i: (b, i, 0))
    outs = pl.pallas_call(
        functools.partial(_rwkv_prep_kernel, lw=lw, la=la, lg=lg),
        grid=(B, T // tt),
        in_specs=[seg(cb, d_b), seg(cb + 1, d_b), seg(cb + 2, d_b), seg(tb, TAIL_BLOCK),
                  halo(cb, d_b), halo(cb + 1, d_b), halo(cb + 2, d_b), halo(tb, TAIL_BLOCK),
                  per_b(d_b), per_b(d_b), per_b(d_b), per_b(TAIL_BLOCK),
                  full((1, d_b)), full((1, d_b)), full((1, d_b)), full((1, TAIL_BLOCK)),
                  full((1, d_b)), full((lw, d_b)), full((1, d_b)), full((la, d_b)), full((lg, d_b)),
                  full((1, d_b)), full((1, d_b))],
        out_specs=[out_row] * 7,
        out_shape=[jax.ShapeDtypeStruct((B, T, d_b), F32)] * 7,
        compiler_params=_params(("parallel", "arbitrary"), 56),
        name="rwkv_prep",
    )(p, p, p, p, p, p, p, p, *s_parts, *m_parts,
      w0[None], w_up.astype(BF16), a0[None], a_up.astype(BF16), g_up.astype(BF16), k_k[None], k_a[None])
    return outs


def _rwkv_scan_kernel(r_ref, w_ref, k_ref, v_ref, kk_ref, a_ref, rk_ref, lnw_ref, lnb_ref, s0_ref,
                      y_ref, so_ref, s_scr, av_scr, bv_scr, *, tc, dh):
    @pl.when(pl.program_id(1) == 0)
    def _():
        s_scr[...] = s0_ref[...]

    def step(t, carry):
        kr = kk_ref[t]
        kk = kr * lax.rsqrt(jnp.maximum(jnp.sum(kr * kr, axis=0, keepdims=True), 1e-24))
        av_scr[...] = -kk
        bv_scr[...] = kk * a_ref[t]
        v = v_ref[t]
        sa = jnp.zeros_like(v)
        for k in range(dh):
            sa = sa + s_scr[k] * av_scr[k:k + 1, :]
        y = jnp.zeros_like(v)
        for k in range(dh):
            s_new = (s_scr[k] * w_ref[t, k:k + 1, :] + sa * bv_scr[k:k + 1, :] + v * k_ref[t, k:k + 1, :])
            s_scr[k] = s_new
            y = y + s_new * r_ref[t, k:k + 1, :]
        mean = jnp.mean(y, axis=0, keepdims=True)
        cen = y - mean
        var = jnp.mean(cen * cen, axis=0, keepdims=True)
        yn = cen * lax.rsqrt(var + RWKV_GN_EPS) * lnw_ref[...] + lnb_ref[...]
        bonus = jnp.sum(r_ref[t] * k_ref[t] * rk_ref[...], axis=0, keepdims=True) * v
        y_ref[t] = yn + bonus
        return carry

    lax.fori_loop(0, tc, step, 0)

    @pl.when(pl.program_id(1) == pl.num_programs(1) - 1)
    def _():
        so_ref[...] = s_scr[...]


def rwkv_scan(r, w, k, v, kk, a, S0, r_k, ln_w, ln_b):
    B, T, d_b = r.shape
    H = d_b // DH_B
    BH = B * H
    BHp = _round_up(BH, LANES)

    def to_lanes(x):
        x = x.reshape(B, T, H, DH_B).transpose(1, 3, 0, 2).reshape(T, DH_B, BH)
        return jnp.pad(x, ((0, 0), (0, 0), (0, BHp - BH)))

    def param_lanes(x):
        x = jnp.tile(x.T, (1, B))
        return jnp.pad(x, ((0, 0), (0, BHp - BH)))

    s0 = S0.transpose(3, 2, 0, 1).reshape(DH_B, DH_B, BH)
    s0 = jnp.pad(s0, ((0, 0), (0, 0), (0, BHp - BH)))
    tc = min(T, 32)
    seq = pl.BlockSpec((tc, DH_B, LANES), lambda l, i: (i, 0, l))
    par = pl.BlockSpec((DH_B, LANES), lambda l, i: (0, l))
    st = pl.BlockSpec((DH_B, DH_B, LANES), lambda l, i: (0, 0, l))
    y, s_fin = pl.pallas_call(
        functools.partial(_rwkv_scan_kernel, tc=tc, dh=DH_B),
        grid=(BHp // LANES, T // tc),
        in_specs=[seq] * 6 + [par] * 3 + [st],
        out_specs=[seq, st],
        out_shape=[jax.ShapeDtypeStruct((T, DH_B, BHp), F32), jax.ShapeDtypeStruct((DH_B, DH_B, BHp), F32)],
        scratch_shapes=[pltpu.VMEM((DH_B, DH_B, LANES), F32), pltpu.VMEM((DH_B, LANES), F32),
                        pltpu.VMEM((DH_B, LANES), F32)],
        compiler_params=_params(("parallel", "arbitrary"), 40),
        name="rwkv_scan",
    )(to_lanes(r), to_lanes(w), to_lanes(k), to_lanes(v), to_lanes(kk), to_lanes(a),
      param_lanes(r_k), param_lanes(ln_w.reshape(H, DH_B)), param_lanes(ln_b.reshape(H, DH_B)), s0)
    y = y[:, :, :BH].reshape(T, DH_B, B, H).transpose(2, 0, 3, 1).reshape(B, T, d_b)
    s_fin = s_fin[:, :, :BH].reshape(DH_B, DH_B, B, H).transpose(2, 3, 1, 0)
    return y, s_fin


def _gate_concat_kernel(oa_ref, yb_ref, g_ref, o_ref, *, d_a):
    o_ref[:, :d_a] = oa_ref[...]
    o_ref[:, d_a:] = (yb_ref[...] * g_ref[...]).astype(o_ref.dtype)


def gate_concat(out_a, yb, g):
    B, T, d_a = out_a.shape
    d_b = yb.shape[-1]
    tt = min(T, 256)
    blk = lambda w: pl.BlockSpec((None, tt, w), lambda b, i: (b, i, 0))
    return pl.pallas_call(
        functools.partial(_gate_concat_kernel, d_a=d_a),
        grid=(B, T // tt),
        in_specs=[blk(d_a), blk(d_b), blk(d_b)],
        out_specs=blk(d_a + d_b),
        out_shape=jax.ShapeDtypeStruct((B, T, d_a + d_b), BF16),
        compiler_params=_params(("parallel", "parallel"), 32),
        name="gate_concat",
    )(out_a, yb, g)


def _rope(x, cos2, sin2):
    return x * cos2 + pltpu.roll(x, x.shape[-1] // 2, axis=1) * sin2


def _retention_kernel(q_ref, k_ref, v_ref, g_ref, cos_ref, sin_ref, dm_ref, kd_ref, qd_ref, gl_ref, gn_ref,
                      r0_ref, o_ref, ro_ref, r_scr, *, L, n_chunks):
    @pl.when(pl.program_id(2) == 0)
    def _():
        r_scr[...] = r0_ref[...]

    scale = DK_C ** -0.5
    for c in range(n_chunks):
        rows = slice(c * L, (c + 1) * L)
        cos2, sin2 = cos_ref[rows, :], sin_ref[rows, :]
        q = _rope(q_ref[rows, :], cos2, sin2)
        k = _rope(k_ref[rows, :], cos2, sin2) * scale
        v = v_ref[rows, :].astype(BF16)
        att = lax.dot_general(q.astype(BF16), k.astype(BF16), (((1,), (1,)), ((), ())),
                              preferred_element_type=F32) * dm_ref[...]
        R = r_scr[...]
        o = jnp.dot(att.astype(BF16), v, preferred_element_type=F32)
        o = o + jnp.dot((q * qd_ref[...]).astype(BF16), R.astype(BF16), preferred_element_type=F32)
        kd_t = (k * kd_ref[...]).T.astype(BF16)
        r_scr[...] = R * gl_ref[...] + jnp.dot(kd_t, v, preferred_element_type=F32)
        mean = jnp.mean(o, axis=-1, keepdims=True)
        cen = o - mean
        var = jnp.mean(cen * cen, axis=-1, keepdims=True)
        on = cen * lax.rsqrt(var + EPS) * gn_ref[...]
        gate = g_ref[rows, :]
        o_ref[rows, :] = (gate * jax.nn.sigmoid(gate) * on).astype(o_ref.dtype)

    @pl.when(pl.program_id(2) == pl.num_programs(2) - 1)
    def _():
        ro_ref[...] = r_scr[...]


def retention(p, h_c, pos, R0, gn_g):
    B, T, _ = p.shape
    L = math.gcd(T, RET_CHUNK)
    tr = min(T, 512)
    n_chunks = tr // L
    half = DK_C // 2
    inv = ROPE_BASE ** (-jnp.arange(half, dtype=F32) / half)
    ang = pos.astype(F32)[:, None] * inv[None, :]
    cos2 = jnp.concatenate([jnp.cos(ang), jnp.cos(ang)], axis=-1)
    sin2 = jnp.concatenate([-jnp.sin(ang), jnp.sin(ang)], axis=-1)
    log_g = jnp.log1p(-jnp.exp2(-5.0 - jnp.arange(h_c, dtype=F32)))
    idx = jnp.arange(L, dtype=F32)
    rel = idx[:, None] - idx[None, :]
    dmat = jnp.where(rel >= 0, jnp.exp(jnp.maximum(rel, 0.0) * log_g[:, None, None]), 0.0)
    k_dec = jnp.exp((L - 1 - idx)[None, :, None] * log_g[:, None, None])
    q_dec = jnp.exp((idx + 1)[None, :, None] * log_g[:, None, None])
    k_dec = jnp.broadcast_to(k_dec, (h_c, L, DK_C))
    q_dec = jnp.broadcast_to(q_dec, (h_c, L, DK_C))
    g_l = jnp.broadcast_to(jnp.exp(L * log_g)[:, None, None], (h_c, 1, DV_C))

    def col(width, c0):
        return pl.BlockSpec((None, tr, width), lambda b, h, i: (b, i, c0 + h))

    def head(shape):
        return pl.BlockSpec((None,) + shape, lambda b, h, i: (h,) + tuple(0 for _ in shape))

    trig = pl.BlockSpec((tr, DK_C), lambda b, h, i: (i, 0))
    state = pl.BlockSpec((None, None, DK_C, DV_C), lambda b, h, i: (b, h, 0, 0))
    return pl.pallas_call(
        functools.partial(_retention_kernel, L=L, n_chunks=n_chunks),
        grid=(B, h_c, T // tr),
        in_specs=[col(DK_C, 0), col(DK_C, h_c), col(DV_C, h_c), col(DV_C, 2 * h_c), trig, trig,
                  head((L, L)), head((L, DK_C)), head((L, DK_C)), head((1, DV_C)),
                  pl.BlockSpec((1, DV_C), lambda b, h, i: (0, h)), state],
        out_specs=[pl.BlockSpec((None, tr, DV_C), lambda b, h, i: (b, i, h)), state],
        out_shape=[jax.ShapeDtypeStruct((B, T, h_c * DV_C), BF16),
                   jax.ShapeDtypeStruct((B, h_c, DK_C, DV_C), F32)],
        scratch_shapes=[pltpu.VMEM((DK_C, DV_C), F32)],
        compiler_params=_params(("parallel", "parallel", "arbitrary"), 32),
        name="retention",
    )(p, p, p, p, cos2, sin2, dmat, k_dec, q_dec, g_l, gn_g.reshape(1, h_c * DV_C), R0)


def _fox_kernel(q_ref, k_ref, v_ref, fq_ref, fk_ref, o_ref, m_scr, l_scr, acc_scr, *, tq, tk, past):
    qi, kj = pl.program_id(2), pl.program_id(3)

    @pl.when(kj == 0)
    def _():
        m_scr[...] = jnp.full_like(m_scr, -jnp.inf)
        l_scr[...] = jnp.zeros_like(l_scr)
        acc_scr[...] = jnp.zeros_like(acc_scr)

    @pl.when(kj * tk <= past + qi * tq + tq - 1)
    def _():
        s = lax.dot_general(q_ref[...].astype(BF16), k_ref[...].astype(BF16), (((1,), (1,)), ((), ())),
                            preferred_element_type=F32)
        s = s * (DH_D ** -0.5) + (fq_ref[...] - fk_ref[...])
        qpos = past + qi * tq + lax.broadcasted_iota(jnp.int32, s.shape, 0)
        kpos = kj * tk + lax.broadcasted_iota(jnp.int32, s.shape, 1)
        s = jnp.where(kpos <= qpos, s, -jnp.inf)
        m_old = m_scr[...]
        m_new = jnp.maximum(m_old, jnp.max(s, axis=-1, keepdims=True))
        alpha = jnp.exp(m_old - m_new)
        pr = jnp.exp(s - m_new)
        l_scr[...] = alpha * l_scr[...] + jnp.sum(pr, axis=-1, keepdims=True)
        acc_scr[...] = alpha * acc_scr[...] + jnp.dot(pr.astype(BF16), v_ref[...].astype(BF16),
                                                      preferred_element_type=F32)
        m_scr[...] = m_new

    @pl.when(kj == pl.num_programs(3) - 1)
    def _():
        o_ref[...] = (acc_scr[...] / l_scr[...]).astype(o_ref.dtype)


def fox_attention(q_src, q_col, k_src, k_col, v_src, v_col, F_all, n_heads, T):
    B = q_src.shape[0]
    Tk = k_src.shape[1]
    past = Tk - T
    tq = min(T, 512)
    tk = Tk if Tk <= 2048 else 512
    assert T % tq == 0 and Tk % tk == 0
    f_row = F_all.transpose(0, 2, 1)[:, :, None, :]
    f_col = F_all[:, past:].transpose(0, 2, 1)[..., None]

    def last_needed(i):
        return (past + i * tq + tq - 1) // tk

    def kv(col0):
        return pl.BlockSpec((None, tk, DH_D), lambda b, h, i, j: (b, jnp.minimum(j, last_needed(i)), col0 + h))

    return pl.pallas_call(
        functools.partial(_fox_kernel, tq=tq, tk=tk, past=past),
        grid=(B, n_heads, T // tq, Tk // tk),
        in_specs=[pl.BlockSpec((None, tq, DH_D), lambda b, h, i, j: (b, i, q_col + h)),
                  kv(k_col), kv(v_col),
                  pl.BlockSpec((None, None, tq, 1), lambda b, h, i, j: (b, h, i, 0)),
                  pl.BlockSpec((None, None, 1, tk), lambda b, h, i, j: (b, h, 0, jnp.minimum(j, last_needed(i))))],
        out_specs=pl.BlockSpec((None, tq, DH_D), lambda b, h, i, j: (b, i, h)),
        out_shape=jax.ShapeDtypeStruct((B, T, n_heads * DH_D), BF16),
        scratch_shapes=[pltpu.VMEM((tq, 1), F32), pltpu.VMEM((tq, 1), F32), pltpu.VMEM((tq, DH_D), F32)],
        compiler_params=_params(("parallel", "parallel", "parallel", "arbitrary"), 32),
        name="fox_attention",
    )(q_src, k_src, v_src, f_col, f_row)


def _top_rows(x, n):
    R = x.shape[0]
    iota = lax.broadcasted_iota(jnp.int32, x.shape, 0)
    vals = []
    work = x
    for _ in range(n):
        m = jnp.max(work, axis=0, keepdims=True)
        vals.append(m)
        first = jnp.min(jnp.where(work == m, iota, R), axis=0, keepdims=True)
        work = jnp.where(iota == first, -jnp.inf, work)
    return vals


def _peer_score_kernel(h_ref, wq_ref, sk_ref, s1_ref, s2_ref, st_ref, *, n_heads, n_keys):
    qT = lax.dot_general(wq_ref[...], h_ref[...], (((1,), (1,)), ((), ())), preferred_element_type=F32)
    dq = qT.shape[0] // (2 * n_heads)
    for hd in range(n_heads):
        tops = []
        for c in range(2):
            hc = 2 * hd + c
            q = qT[hc * dq:(hc + 1) * dq, :].astype(BF16)
            s = jnp.dot(sk_ref[hc], q, preferred_element_type=F32)
            (s1_ref if c == 0 else s2_ref)[hd] = s
            tops.append(_top_rows(s, TOPK_P))
        v2 = jnp.concatenate(tops[1], axis=0)
        cand = jnp.concatenate([tops[0][a] + v2 for a in range(TOPK_P)], axis=0)
        sc = _top_rows(cand, TOPK_P)
        z = sc[0] - sc[0]
        for s_k in sc:
            z = z + jnp.exp(s_k - sc[0])
        st_ref[hd, 0:1, :] = sc[TOPK_P - 1]
        st_ref[hd, 1:2, :] = sc[0] + jnp.log(z)
        st_ref[hd, 2:8, :] = jnp.zeros((6, z.shape[1]), F32)


def peer_scores(h2d, wq_t, subkeys_b):
    N, d = h2d.shape
    n_hc, n_keys, dq = subkeys_b.shape
    n_heads = n_hc // 2
    tt = min(N, 256)
    return pl.pallas_call(
        functools.partial(_peer_score_kernel, n_heads=n_heads, n_keys=n_keys),
        grid=(N // tt,),
        in_specs=[pl.BlockSpec((tt, d), lambda i: (i, 0)),
                  pl.BlockSpec(wq_t.shape, lambda i: (0, 0), pipeline_mode=pl.Buffered(1)),
                  pl.BlockSpec(subkeys_b.shape, lambda i: (0, 0, 0))],
        out_specs=[pl.BlockSpec((n_heads, n_keys, tt), lambda i: (0, 0, i)),
                   pl.BlockSpec((n_heads, n_keys, tt), lambda i: (0, 0, i)),
                   pl.BlockSpec((n_heads, 8, tt), lambda i: (0, 0, i))],
        out_shape=[jax.ShapeDtypeStruct((n_heads, n_keys, N), F32),
                   jax.ShapeDtypeStruct((n_heads, n_keys, N), F32),
                   jax.ShapeDtypeStruct((n_heads, 8, N), F32)],
        compiler_params=_params(("parallel",), 48),
        name="peer_scores",
    )(h2d, wq_t, subkeys_b)


def _peer_expert_kernel(h_ref, u_ref, vt_ref, s1_ref, s2_ref, st_ref, o_ref, *, n_heads, n_keys, rows):
    j = pl.program_id(1)

    @pl.when(j == 0)
    def _():
        o_ref[...] = jnp.zeros_like(o_ref)

    hT = lax.dot_general(u_ref[...], h_ref[...], (((1,), (1,)), ((), ())), preferred_element_type=F32)
    act = jax.nn.gelu(hT)
    gates = []
    for r in range(rows):
        g = jnp.zeros((n_keys, act.shape[1]), F32)
        for hd in range(n_heads):
            s = s1_ref[hd, pl.ds(j * rows + r, 1), :] + s2_ref[hd]
            g = g + jnp.where(s >= st_ref[hd, 0:1, :], jnp.exp(s - st_ref[hd, 1:2, :]), 0.0)
        gates.append(g)
    w = (act * jnp.concatenate(gates, axis=0)).astype(BF16)
    o_ref[...] += jnp.dot(vt_ref[...], w, preferred_element_type=F32)


def peer_experts(h2d, u_b, v_t, s1, s2, st):
    N, d = h2d.shape
    n_exp = u_b.shape[0]
    n_heads, n_keys, _ = s1.shape
    tt = min(N, 512)
    te = 512
    rows = te // n_keys
    once = pl.Buffered(1)
    return pl.pallas_call(
        functools.partial(_peer_expert_kernel, n_heads=n_heads, n_keys=n_keys, rows=rows),
        grid=(N // tt, n_exp // te),
        in_specs=[pl.BlockSpec((tt, d), lambda i, j: (i, 0), pipeline_mode=once),
                  pl.BlockSpec((te, d), lambda i, j: (j, 0)),
                  pl.BlockSpec((d, te), lambda i, j: (0, j)),
                  pl.BlockSpec((n_heads, n_keys, tt), lambda i, j: (0, 0, i), pipeline_mode=once),
                  pl.BlockSpec((n_heads, n_keys, tt), lambda i, j: (0, 0, i), pipeline_mode=once),
                  pl.BlockSpec((n_heads, 8, tt), lambda i, j: (0, 0, i), pipeline_mode=once)],
        out_specs=pl.BlockSpec((d, tt), lambda i, j: (0, i)),
        out_shape=jax.ShapeDtypeStruct((d, N), F32),
        compiler_params=_params(("parallel", "arbitrary"), 56),
        name="peer_experts",
    )(h2d, u_b, v_t, s1, s2, st)


def peer_ffn(h, wq_t, subkeys_b, u_b, v_t):
    B, T, d = h.shape
    h2d = h.reshape(B * T, d)
    s1, s2, st = peer_scores(h2d, wq_t, subkeys_b)
    return peer_experts(h2d, u_b, v_t, s1, s2, st)


def _pad_cols(w, n):
    return jnp.pad(w, ((0, 0), (0, n - w.shape[1])))


def kernel(x_prompt, x_sample, c_prompt, c_sample, state_rwkv_shift, state_rwkv, state_ret, cache_fox_k, cache_fox_v, cache_fox_logf, ada_w, ada_b, norm_g, final_g, w_in_even, w_out_even, sgu_norm_g, sgu_w, sgu_b, rwkv_mu, rwkv_w0, rwkv_w_up, rwkv_a0, rwkv_a_up, rwkv_g_up, rwkv_k_k, rwkv_k_a, rwkv_r_k, rwkv_ln_w, rwkv_ln_b, w_in_odd, w_out_odd, ret_gn_g, fox_b_f, peer_wq, peer_subkeys, peer_u, peer_v):
    depth = ada_w.shape[0]
    d = x_prompt.shape[-1]
    d_a = d_b = d // 2
    h_b = d_b // DH_B
    h_c = d // (2 * DV_C)
    h_d = d // (2 * DH_D)
    p_b = rwkv_mu.shape[1]
    n_bp, n_bs = x_prompt.shape[0], x_sample.shape[0]
    past_len = cache_fox_k.shape[2]

    n_rows = _round_up(n_bp + n_bs, 16)
    c_all = jnp.pad(jnp.concatenate([c_prompt, c_sample], axis=0), ((0, n_rows - n_bp - n_bs), (0, 0)))
    mod4 = ada_modulation(c_all, ada_w, ada_b).reshape(depth, n_rows, 1, 6 * d)

    w_in_e = [_pad_cols(w_in_even[j], 2 * d_a + 3 * d_b + TAIL_BLOCK).astype(BF16) for j in range(w_in_even.shape[0])]
    w_in_o = [_pad_cols(w_in_odd[j], _round_up(w_in_odd.shape[2], 512)).astype(BF16) for j in range(w_in_odd.shape[0])]
    w_out_e = [w_out_even[j].astype(BF16) for j in range(w_out_even.shape[0])]
    w_out_o = [w_out_odd[j].astype(BF16) for j in range(w_out_odd.shape[0])]
    n_hp = peer_subkeys.shape[1]
    wq_t = [peer_wq[i].T.astype(BF16) for i in range(depth)]
    subk = [peer_subkeys[i].reshape(2 * n_hp, peer_subkeys.shape[3], peer_subkeys.shape[4]).astype(BF16)
            for i in range(depth)]
    u_b = [peer_u[i].astype(BF16) for i in range(depth)]
    v_t = [peer_v[i].T.astype(BF16) for i in range(depth)]

    def run(x, row0, pos0, shift0, rwkv0, ret0, fox_past):
        B, T, _ = x.shape
        pos = pos0 + jnp.arange(T)
        v_rows, shifts, rwkvs, rets, ks, vs, lfs = [], [], [], [], [], [], []
        pending = None
        for i in range(depth):
            j = i // 2
            res = None if pending is None else (pending[0], pending[1], mod4, pending[3], row0, pending[2])
            x_new, h = resnorm(x, norm_g[i, 0], res=res, mod=(mod4, i, row0, 1, 0))
            x = x if x_new is None else x_new
            h2d = h.reshape(B * T, d)
            if i % 2 == 0:
                p = matmul(h2d, w_in_e[j]).reshape(B, T, -1)
                out_a, vn = spatial_gating(p, d_a, sgu_norm_g[j], sgu_w[j], sgu_b[j])
                r, w, k, v, kk, a, g = rwkv_prep(p, d_b, 2 * d_a, shift0[j], rwkv_mu[j], rwkv_w0[j], rwkv_w_up[j],
                                                 rwkv_a0[j], rwkv_a_up[j], rwkv_g_up[j], rwkv_k_k[j], rwkv_k_a[j])
                yb, S = rwkv_scan(r, w, k, v, kk, a, rwkv0[j], rwkv_r_k[j], rwkv_ln_w[j], rwkv_ln_b[j])
                mix = gate_concat(out_a, yb, g)
                y = matmul(mix.reshape(B * T, d), w_out_e[j]).reshape(B, T, d)
                v_rows.append(vn)
                shifts.append(p[:, -1:, 2 * d_a:2 * d_a + p_b])
                rwkvs.append(S)
            else:
                p = matmul(h2d, w_in_o[j]).reshape(B, T, -1)
                c_dq = 2 * h_c * DK_C + 2 * h_c * DV_C
                c_dk, c_dv, c_df = c_dq + h_d * DH_D, c_dq + 2 * h_d * DH_D, c_dq + 3 * h_d * DH_D
                out_c, R = retention(p, h_c, pos, ret0[j], ret_gn_g[j])
                logf = jax.nn.log_sigmoid(p[..., c_df:c_df + h_d] + fox_b_f[j])
                kd = p[..., c_dk:c_dk + h_d * DH_D]
                vd = p[..., c_dv:c_dv + h_d * DH_D]
                if fox_past is None:
                    F_all = jnp.cumsum(logf, axis=1)
                    od = fox_attention(p, c_dq // DH_D, p, c_dk // DH_D, p, c_dv // DH_D, F_all, h_d, T)
                else:
                    k_past, v_past, lf_past = fox_past[j]
                    k_all = jnp.concatenate([k_past.reshape(B, past_len, -1), kd], axis=1)
                    v_all = jnp.concatenate([v_past.reshape(B, past_len, -1), vd], axis=1)
                    F_all = jnp.cumsum(jnp.concatenate([lf_past, logf], axis=1), axis=1)
                    od = fox_attention(p, c_dq // DH_D, k_all, 0, v_all, 0, F_all, h_d, T)
                mix = jnp.concatenate([out_c, od], axis=-1)
                y = matmul(mix.reshape(B * T, d), w_out_o[j]).reshape(B, T, d)
                rets.append(R)
                ks.append(kd.reshape(B, T, h_d, DH_D))
                vs.append(vd.reshape(B, T, h_d, DH_D))
                lfs.append(logf)
            x, h = resnorm(x, norm_g[i, 1], res=(y, False, mod4, i, row0, 2), mod=(mod4, i, row0, 4, 3))
            y_t = peer_ffn(h, wq_t[i], subk[i], u_b[i], v_t[i])
            if T % LANES == 0:
                pending = (y_t, True, 5, i)
            else:
                pending = (y_t.T.reshape(B, T, d), False, 5, i)
        _, out = resnorm(x, final_g, res=(pending[0], pending[1], mod4, pending[3], row0, pending[2]),
                         out_dtype=F32)
        return out, v_rows, shifts, rwkvs, rets, ks, vs, lfs

    n_even, n_odd = w_in_even.shape[0], w_in_odd.shape[0]
    y_p, _, sh_p, rw_p, rt_p, k_p, v_p, lf_p = run(
        x_prompt, 0, 0,
        [jnp.zeros((n_bp, 1, p_b), F32)] * n_even,
        [jnp.zeros((n_bp, h_b, DH_B, DH_B), F32)] * n_even,
        [jnp.zeros((n_bp, h_c, DK_C, DV_C), F32)] * n_odd,
        None)
    y_s, vr_s, sh_s, rw_s, rt_s, k_s, v_s, lf_s = run(
        x_sample, n_bp, past_len,
        [state_rwkv_shift[j] for j in range(n_even)],
        [state_rwkv[j] for j in range(n_even)],
        [state_ret[j] for j in range(n_odd)],
        [(cache_fox_k[j], cache_fox_v[j], cache_fox_logf[j]) for j in range(n_odd)])
    return (y_p, y_s, jnp.stack(vr_s), jnp.stack(sh_p), jnp.stack(sh_s), jnp.stack(rw_p), jnp.stack(rw_s),
            jnp.stack(rt_p), jnp.stack(rt_s), jnp.stack(k_p), jnp.stack(k_s), jnp.stack(v_p), jnp.stack(v_s),
            jnp.stack(lf_p), jnp.stack(lf_s))
```

```python
import functools
import math

import jax
import jax.numpy as jnp
import numpy as np
from jax import lax
from jax.experimental import pallas as pl
from jax.experimental.pallas import tpu as pltpu

F32 = jnp.float32
BF16 = jnp.bfloat16

EPS = 1e-6
RWKV_GN_EPS = 64e-5
ROPE_BASE = 10000.0
RET_CHUNK = 64
MLP_CHUNK = 128
N_GROUPS_A = 8
DH_B = 64
DK_C = 128
DV_C = 256
DH_D = 128
TOPK_P = 16

LANES = 128
TAIL_BLOCK = 512
MIB = 1024 * 1024


def _round_up(n, m):
    return (n + m - 1) // m * m


def _params(sem, vmem_mib):
    return pltpu.CompilerParams(dimension_semantics=sem, vmem_limit_bytes=int(vmem_mib * MIB))


def _mm_kernel(a_ref, b_ref, o_ref):
    o_ref[...] = jnp.dot(a_ref[...], b_ref[...], preferred_element_type=F32).astype(o_ref.dtype)


def matmul(a, b, out_dtype=F32):
    M, K = a.shape
    N = b.shape[1]
    tm = min(M, 1024)
    tn = 512
    assert M % tm == 0 and N % tn == 0, (M, N)
    return pl.pallas_call(
        _mm_kernel,
        grid=(M // tm, N // tn),
        in_specs=[pl.BlockSpec((tm, K), lambda i, j: (i, 0)),
                  pl.BlockSpec((K, tn), lambda i, j: (0, j))],
        out_specs=pl.BlockSpec((tm, tn), lambda i, j: (i, j)),
        out_shape=jax.ShapeDtypeStruct((M, N), out_dtype),
        compiler_params=_params(("parallel", "parallel"), 48),
        name="matmul",
    )(a, b)


def _ada_kernel(c_ref, w_ref, b_ref, o_ref):
    c = c_ref[...]
    a = (c * jax.nn.sigmoid(c)).astype(BF16)
    o_ref[...] = jnp.dot(a, w_ref[...].astype(BF16), preferred_element_type=F32) + b_ref[...]


def ada_modulation(c_all, ada_w, ada_b):
    depth, d, n6 = ada_w.shape
    R = c_all.shape[0]
    tn = 512
    return pl.pallas_call(
        _ada_kernel,
        grid=(depth, n6 // tn),
        in_specs=[pl.BlockSpec((R, d), lambda i, j: (0, 0)),
                  pl.BlockSpec((None, d, tn), lambda i, j: (i, 0, j)),
                  pl.BlockSpec((None, 1, tn), lambda i, j: (i, 0, j))],
        out_specs=pl.BlockSpec((None, R, tn), lambda i, j: (i, 0, j)),
        out_shape=jax.ShapeDtypeStruct((depth, R, n6), F32),
        compiler_params=_params(("parallel", "parallel"), 40),
        name="ada_modulation",
    )(c_all, ada_w, ada_b.reshape(depth, 1, n6))


def _resnorm_kernel(*refs, has_res, y_transposed, has_mod):
    it = iter(refs)
    x_ref = next(it)
    if has_res:
        y_ref = next(it)
        gate_ref = next(it)
    g_ref = next(it)
    if has_mod:
        sc_ref = next(it)
        sh_ref = next(it)
    if has_res:
        xo_ref = next(it)
    h_ref = next(it)
    x = x_ref[...]
    if has_res:
        y = y_ref[...]
        if y_transposed:
            y = y.T
        x = x + gate_ref[...] * y
        xo_ref[...] = x
    xn = x * lax.rsqrt(jnp.mean(x * x, axis=-1, keepdims=True) + EPS) * g_ref[...]
    if has_mod:
        xn = xn * (1.0 + sc_ref[...]) + sh_ref[...]
    h_ref[...] = xn.astype(h_ref.dtype)


def resnorm(x, norm_g, *, res=None, mod=None, out_dtype=BF16):
    B, T, d = x.shape
    tt = min(T, 256)
    nt = T // tt
    row = pl.BlockSpec((None, tt, d), lambda b, i: (b, i, 0))
    args, specs = [x], [row]
    if res is not None:
        y, y_transposed, mod4, layer, row0, gk = res
        args.append(y)
        if y_transposed:
            specs.append(pl.BlockSpec((d, tt), lambda b, i: (0, b * nt + i)))
        else:
            specs.append(row)
        args.append(mod4)
        specs.append(pl.BlockSpec((None, None, 1, d), lambda b, i: (layer, row0 + b, 0, gk)))
    else:
        y_transposed = False
    args.append(norm_g.reshape(1, d))
    specs.append(pl.BlockSpec((1, d), lambda b, i: (0, 0)))
    if mod is not None:
        mod4m, layer_m, row0_m, sck, shk = mod
        args += [mod4m, mod4m]
        specs.append(pl.BlockSpec((None, None, 1, d), lambda b, i: (layer_m, row0_m + b, 0, sck)))
        specs.append(pl.BlockSpec((None, None, 1, d), lambda b, i: (layer_m, row0_m + b, 0, shk)))
    out_shape, out_specs = [], []
    if res is not None:
        out_shape.append(jax.ShapeDtypeStruct((B, T, d), F32))
        out_specs.append(row)
    out_shape.append(jax.ShapeDtypeStruct((B, T, d), out_dtype))
    out_specs.append(row)
    outs = pl.pallas_call(
        functools.partial(_resnorm_kernel, has_res=res is not None, y_transposed=y_transposed,
                          has_mod=mod is not None),
        grid=(B, nt),
        in_specs=specs,
        out_specs=out_specs,
        out_shape=out_shape,
        compiler_params=_params(("parallel", "parallel"), 48),
        name="resnorm",
    )(*args)
    if res is not None:
        return outs[0], outs[1]
    return None, outs[0]


def _sgu_kernel(pu_ref, pv_ref, ng_ref, w_ref, b_ref, oa_ref, vn_ref, *, groups):
    u = jax.nn.gelu(pu_ref[...])
    v = jax.nn.gelu(pv_ref[...])
    dg = v.shape[1] // groups
    for g in range(groups):
        sl = slice(g * dg, (g + 1) * dg)
        vg = v[:, sl]
        mean = jnp.mean(vg, axis=-1, keepdims=True)
        cen = vg - mean
        var = jnp.mean(cen * cen, axis=-1, keepdims=True)
        vn = cen * lax.rsqrt(var + EPS) * ng_ref[:, sl]
        vn_ref[:, sl] = vn
        s = jnp.dot(w_ref[g], vn.astype(BF16), preferred_element_type=F32) + b_ref[:, sl]
        oa_ref[:, sl] = (u[:, sl] * s).astype(oa_ref.dtype)


def spatial_gating(p, d_a, norm_g, w_s, b_s):
    B, T, _ = p.shape
    L = min(T, MLP_CHUNK)
    G = w_s.shape[0]
    dg = d_a // G
    w_tril = jnp.tril(w_s)[:, :L, :L].astype(BF16)
    b_full = jnp.repeat(b_s.T[:L], dg, axis=1)
    blk = lambda c: pl.BlockSpec((None, L, d_a), lambda b, i: (b, i, c))
    return pl.pallas_call(
        functools.partial(_sgu_kernel, groups=G),
        grid=(B, T // L),
        in_specs=[blk(0), blk(1),
                  pl.BlockSpec((1, d_a), lambda b, i: (0, 0)),
                  pl.BlockSpec((G, L, L), lambda b, i: (0, 0, 0)),
                  pl.BlockSpec((L, d_a), lambda b, i: (0, 0))],
        out_specs=[blk(0), blk(0)],
        out_shape=[jax.ShapeDtypeStruct((B, T, d_a), BF16), jax.ShapeDtypeStruct((B, T, d_a), F32)],
        compiler_params=_params(("parallel", "parallel"), 32),
        name="spatial_gating",
    )(p, p, norm_g.reshape(1, d_a), w_tril, b_full)


def _shifted(x_ref, halo_ref, state_ref, mu_ref):
    x = x_ref[...]
    first = jnp.where(pl.program_id(1) == 0, state_ref[...], halo_ref[7:8, :])
    rows = lax.broadcasted_iota(jnp.int32, x.shape, 0)
    prev = jnp.where(rows == 0, first, pltpu.roll(x, 1, axis=0))
    return x + (prev - x) * mu_ref[...]


def _rwkv_prep_kernel(r_ref, k_ref, v_ref, t_ref, rh_ref, kh_ref, vh_ref, th_ref,
                      sr_ref, sk_ref, sv_ref, st_ref, mr_ref, mk_ref, mv_ref, mt_ref,
                      w0_ref, wup_ref, a0_ref, aup_ref, gup_ref, kk_ref, ka_ref,
                      ro_ref, wo_ref, ko_ref, vo_ref, kko_ref, ao_ref, go_ref, *, lw, la, lg):
    r = _shifted(r_ref, rh_ref, sr_ref, mr_ref)
    k = _shifted(k_ref, kh_ref, sk_ref, mk_ref)
    v = _shifted(v_ref, vh_ref, sv_ref, mv_ref)
    t = _shifted(t_ref, th_ref, st_ref, mt_ref)
    wd, ad, gd = t[:, :lw], t[:, lw:lw + la], t[:, lw + la:lw + la + lg]
    wl = w0_ref[...] + jnp.dot(jnp.tanh(wd).astype(BF16), wup_ref[...], preferred_element_type=F32)
    w_log = -jax.nn.softplus(-wl) - 0.5
    wo_ref[...] = jnp.exp(-jnp.exp(w_log))
    a = jax.nn.sigmoid(a0_ref[...] + jnp.dot(ad.astype(BF16), aup_ref[...], preferred_element_type=F32))
    go_ref[...] = jnp.dot(jax.nn.sigmoid(gd).astype(BF16), gup_ref[...], preferred_element_type=F32)
    ro_ref[...] = r
    vo_ref[...] = v
    ao_ref[...] = a
    kko_ref[...] = k * kk_ref[...]
    ko_ref[...] = k * (1.0 + (a - 1.0) * ka_ref[...])


def rwkv_prep(p, d_b, col0, shift_state, mu, w0, w_up, a0, a_up, g_up, k_k, k_a):
    B, T, _ = p.shape
    lw, la, lg = w_up.shape[0], a_up.shape[0], g_up.shape[0]
    tt = min(T, 128)
    cb = col0 // d_b
    tb = (col0 + 3 * d_b) // TAIL_BLOCK
    assert col0 % d_b == 0 and (col0 + 3 * d_b) % TAIL_BLOCK == 0 and lw + la + lg <= TAIL_BLOCK
    pad = TAIL_BLOCK - (lw + la + lg)

    def seg(c, width):
        return pl.BlockSpec((None, tt, width), lambda b, i: (b, i, c))

    def halo(c, width):
        return pl.BlockSpec((None, 8, width), lambda b, i: (b, jnp.maximum(i * (tt // 8) - 1, 0), c))

    def per_b(width):
        return pl.BlockSpec((None, 1, width), lambda b, i: (b, 0, 0))

    def full(shape):
        return pl.BlockSpec(shape, lambda b, i: tuple(0 for _ in shape))

    st = shift_state
    s_parts = [st[..., :d_b], st[..., d_b:2 * d_b], st[..., 2 * d_b:3 * d_b],
               jnp.pad(st[..., 3 * d_b:], ((0, 0), (0, 0), (0, pad)))]
    m_parts = [mu[None, :d_b], mu[None, d_b:2 * d_b], mu[None, 2 * d_b:3 * d_b],
               jnp.pad(mu[None, 3 * d_b:], ((0, 0), (0, pad)))]
    out_row = pl.BlockSpec((None, tt, d_b), lambda b, i: (b, i, 0))
    outs = pl.pallas_call(
        functools.partial(_rwkv_prep_kernel, lw=lw, la=la, lg=lg),
        grid=(B, T // tt),
        in_specs=[seg(cb, d_b), seg(cb + 1, d_b), seg(cb + 2, d_b), seg(tb, TAIL_BLOCK),
                  halo(cb, d_b), halo(cb + 1, d_b), halo(cb + 2, d_b), halo(tb, TAIL_BLOCK),
                  per_b(d_b), per_b(d_b), per_b(d_b), per_b(TAIL_BLOCK),
                  full((1, d_b)), full((1, d_b)), full((1, d_b)), full((1, TAIL_BLOCK)),
                  full((1, d_b)), full((lw, d_b)), full((1, d_b)), full((la, d_b)), full((lg, d_b)),
                  full((1, d_b)), full((1, d_b))],
        out_specs=[out_row] * 7,
        out_shape=[jax.ShapeDtypeStruct((B, T, d_b), F32)] * 7,
        compiler_params=_params(("parallel", "arbitrary"), 56),
        name="rwkv_prep",
    )(p, p, p, p, p, p, p, p, *s_parts, *m_parts,
      w0[None], w_up.astype(BF16), a0[None], a_up.astype(BF16), g_up.astype(BF16), k_k[None], k_a[None])
    return outs


def _rwkv_scan_kernel(r_ref, w_ref, k_ref, v_ref, kk_ref, a_ref, rk_ref, lnw_ref, lnb_ref, s0_ref,
                      y_ref, so_ref, s_scr, av_scr, bv_scr, *, tc, dh):
    @pl.when(pl.program_id(1) == 0)
    def _():
        s_scr[...] = s0_ref[...]

    def step(t, carry):
        kr = kk_ref[t]
        kk = kr * lax.rsqrt(jnp.maximum(jnp.sum(kr * kr, axis=0, keepdims=True), 1e-24))
        av_scr[...] = -kk
        bv_scr[...] = kk * a_ref[t]
        v = v_ref[t]
        sa = jnp.zeros_like(v)
        for k in range(dh):
            sa = sa + s_scr[k] * av_scr[k:k + 1, :]
        y = jnp.zeros_like(v)
        for k in range(dh):
            s_new = (s_scr[k] * w_ref[t, k:k + 1, :] + sa * bv_scr[k:k + 1, :] + v * k_ref[t, k:k + 1, :])
            s_scr[k] = s_new
            y = y + s_new * r_ref[t, k:k + 1, :]
        mean = jnp.mean(y, axis=0, keepdims=True)
        cen = y - mean
        var = jnp.mean(cen * cen, axis=0, keepdims=True)
        yn = cen * lax.rsqrt(var + RWKV_GN_EPS) * lnw_ref[...] + lnb_ref[...]
        bonus = jnp.sum(r_ref[t] * k_ref[t] * rk_ref[...], axis=0, keepdims=True) * v
        y_ref[t] = yn + bonus
        return carry

    lax.fori_loop(0, tc, step, 0)

    @pl.when(pl.program_id(1) == pl.num_programs(1) - 1)
    def _():
        so_ref[...] = s_scr[...]


def rwkv_scan(r, w, k, v, kk, a, S0, r_k, ln_w, ln_b):
    B, T, d_b = r.shape
    H = d_b // DH_B
    BH = B * H
    BHp = _round_up(BH, LANES)

    def to_lanes(x):
        x = x.reshape(B, T, H, DH_B).transpose(1, 3, 0, 2).reshape(T, DH_B, BH)
        return jnp.pad(x, ((0, 0), (0, 0), (0, BHp - BH)))

    def param_lanes(x):
        x = jnp.tile(x.T, (1, B))
        return jnp.pad(x, ((0, 0), (0, BHp - BH)))

    s0 = S0.transpose(3, 2, 0, 1).reshape(DH_B, DH_B, BH)
    s0 = jnp.pad(s0, ((0, 0), (0, 0), (0, BHp - BH)))
    tc = min(T, 32)
    seq = pl.BlockSpec((tc, DH_B, LANES), lambda l, i: (i, 0, l))
    par = pl.BlockSpec((DH_B, LANES), lambda l, i: (0, l))
    st = pl.BlockSpec((DH_B, DH_B, LANES), lambda l, i: (0, 0, l))
    y, s_fin = pl.pallas_call(
        functools.partial(_rwkv_scan_kernel, tc=tc, dh=DH_B),
        grid=(BHp // LANES, T // tc),
        in_specs=[seq] * 6 + [par] * 3 + [st],
        out_specs=[seq, st],
        out_shape=[jax.ShapeDtypeStruct((T, DH_B, BHp), F32), jax.ShapeDtypeStruct((DH_B, DH_B, BHp), F32)],
        scratch_shapes=[pltpu.VMEM((DH_B, DH_B, LANES), F32), pltpu.VMEM((DH_B, LANES), F32),
                        pltpu.VMEM((DH_B, LANES), F32)],
        compiler_params=_params(("parallel", "arbitrary"), 40),
        name="rwkv_scan",
    )(to_lanes(r), to_lanes(w), to_lanes(k), to_lanes(v), to_lanes(kk), to_lanes(a),
      param_lanes(r_k), param_lanes(ln_w.reshape(H, DH_B)), param_lanes(ln_b.reshape(H, DH_B)), s0)
    y = y[:, :, :BH].reshape(T, DH_B, B, H).transpose(2, 0, 3, 1).reshape(B, T, d_b)
    s_fin = s_fin[:, :, :BH].reshape(DH_B, DH_B, B, H).transpose(2, 3, 1, 0)
    return y, s_fin


def _gate_concat_kernel(oa_ref, yb_ref, g_ref, o_ref, *, d_a):
    o_ref[:, :d_a] = oa_ref[...]
    o_ref[:, d_a:] = (yb_ref[...] * g_ref[...]).astype(o_ref.dtype)


def gate_concat(out_a, yb, g):
    B, T, d_a = out_a.shape
    d_b = yb.shape[-1]
    tt = min(T, 256)
    blk = lambda w: pl.BlockSpec((None, tt, w), lambda b, i: (b, i, 0))
    return pl.pallas_call(
        functools.partial(_gate_concat_kernel, d_a=d_a),
        grid=(B, T // tt),
        in_specs=[blk(d_a), blk(d_b), blk(d_b)],
        out_specs=blk(d_a + d_b),
        out_shape=jax.ShapeDtypeStruct((B, T, d_a + d_b), BF16),
        compiler_params=_params(("parallel", "parallel"), 32),
        name="gate_concat",
    )(out_a, yb, g)


def _rope(x, cos2, sin2):
    return x * cos2 + pltpu.roll(x, x.shape[-1] // 2, axis=1) * sin2


def _retention_kernel(q_ref, k_ref, v_ref, g_ref, cos_ref, sin_ref, dm_ref, kd_ref, qd_ref, gl_ref, gn_ref,
                      r0_ref, o_ref, ro_ref, r_scr, *, L, n_chunks):
    @pl.when(pl.program_id(2) == 0)
    def _():
        r_scr[...] = r0_ref[...]

    scale = DK_C ** -0.5
    for c in range(n_chunks):
        rows = slice(c * L, (c + 1) * L)
        cos2, sin2 = cos_ref[rows, :], sin_ref[rows, :]
        q = _rope(q_ref[rows, :], cos2, sin2)
        k = _rope(k_ref[rows, :], cos2, sin2) * scale
        v = v_ref[rows, :].astype(BF16)
        att = lax.dot_general(q.astype(BF16), k.astype(BF16), (((1,), (1,)), ((), ())),
                              preferred_element_type=F32) * dm_ref[...]
        R = r_scr[...]
        o = jnp.dot(att.astype(BF16), v, preferred_element_type=F32)
        o = o + jnp.dot((q * qd_ref[...]).astype(BF16), R.astype(BF16), preferred_element_type=F32)
        kd_t = (k * kd_ref[...]).T.astype(BF16)
        r_scr[...] = R * gl_ref[...] + jnp.dot(kd_t, v, preferred_element_type=F32)
        mean = jnp.mean(o, axis=-1, keepdims=True)
        cen = o - mean
        var = jnp.mean(cen * cen, axis=-1, keepdims=True)
        on = cen * lax.rsqrt(var + EPS) * gn_ref[...]
        gate = g_ref[rows, :]
        o_ref[rows, :] = (gate * jax.nn.sigmoid(gate) * on).astype(o_ref.dtype)

    @pl.when(pl.program_id(2) == pl.num_programs(2) - 1)
    def _():
        ro_ref[...] = r_scr[...]


def retention(p, h_c, pos, R0, gn_g):
    B, T, _ = p.shape
    L = math.gcd(T, RET_CHUNK)
    tr = min(T, 512)
    n_chunks = tr // L
    half = DK_C // 2
    inv = ROPE_BASE ** (-jnp.arange(half, dtype=F32) / half)
    ang = pos.astype(F32)[:, None] * inv[None, :]
    cos2 = jnp.concatenate([jnp.cos(ang), jnp.cos(ang)], axis=-1)
    sin2 = jnp.concatenate([-jnp.sin(ang), jnp.sin(ang)], axis=-1)
    log_g = jnp.log1p(-jnp.exp2(-5.0 - jnp.arange(h_c, dtype=F32)))
    idx = jnp.arange(L, dtype=F32)
    rel = idx[:, None] - idx[None, :]
    dmat = jnp.where(rel >= 0, jnp.exp(jnp.maximum(rel, 0.0) * log_g[:, None, None]), 0.0)
    k_dec = jnp.exp((L - 1 - idx)[None, :, None] * log_g[:, None, None])
    q_dec = jnp.exp((idx + 1)[None, :, None] * log_g[:, None, None])
    k_dec = jnp.broadcast_to(k_dec, (h_c, L, DK_C))
    q_dec = jnp.broadcast_to(q_dec, (h_c, L, DK_C))
    g_l = jnp.broadcast_to(jnp.exp(L * log_g)[:, None, None], (h_c, 1, DV_C))

    def col(width, c0):
        return pl.BlockSpec((None, tr, width), lambda b, h, i: (b, i, c0 + h))

    def head(shape):
        return pl.BlockSpec((None,) + shape, lambda b, h, i: (h,) + tuple(0 for _ in shape))

    trig = pl.BlockSpec((tr, DK_C), lambda b, h, i: (i, 0))
    state = pl.BlockSpec((None, None, DK_C, DV_C), lambda b, h, i: (b, h, 0, 0))
    return pl.pallas_call(
        functools.partial(_retention_kernel, L=L, n_chunks=n_chunks),
        grid=(B, h_c, T // tr),
        in_specs=[col(DK_C, 0), col(DK_C, h_c), col(DV_C, h_c), col(DV_C, 2 * h_c), trig, trig,
                  head((L, L)), head((L, DK_C)), head((L, DK_C)), head((1, DV_C)),
                  pl.BlockSpec((1, DV_C), lambda b, h, i: (0, h)), state],
        out_specs=[pl.BlockSpec((None, tr, DV_C), lambda b, h, i: (b, i, h)), state],
        out_shape=[jax.ShapeDtypeStruct((B, T, h_c * DV_C), BF16),
                   jax.ShapeDtypeStruct((B, h_c, DK_C, DV_C), F32)],
        scratch_shapes=[pltpu.VMEM((DK_C, DV_C), F32)],
        compiler_params=_params(("parallel", "parallel", "arbitrary"), 32),
        name="retention",
    )(p, p, p, p, cos2, sin2, dmat, k_dec, q_dec, g_l, gn_g.reshape(1, h_c * DV_C), R0)


def _fox_kernel(q_ref, k_ref, v_ref, fq_ref, fk_ref, o_ref, kb_scr, vt_scr, acc_scr, *, tq, tkc, past):
    qi = pl.program_id(2)
    n_kv = k_ref.shape[0] // tkc

    @pl.when(qi == 0)
    def _():
        for c in range(n_kv):
            rows = slice(c * tkc, (c + 1) * tkc)
            kb_scr[rows, :] = k_ref[rows, :].astype(BF16)
            vt_scr[:, rows] = v_ref[rows, :].T.astype(BF16)

    q = q_ref[...].astype(BF16)
    fq = fq_ref[...]
    scale = DH_D ** -0.5
    n_rep = max(tq // LANES, 1)

    def chunk(c, carry, masked):
        m_old, l_old = carry
        r0 = 0 if n_kv == 1 else pl.multiple_of(c * tkc, tkc)
        s = lax.dot_general(kb_scr[pl.ds(r0, tkc), :], q, (((1,), (1,)), ((), ())), preferred_element_type=F32)
        fk = fk_ref[pl.ds(r0, tkc), :]
        fk = jnp.concatenate([fk] * n_rep, axis=1) if tq >= LANES else fk[:, :tq]
        s = s * scale + (fq - fk)
        if masked:
            kpos = r0 + lax.broadcasted_iota(jnp.int32, s.shape, 0)
            qpos = past + qi * tq + lax.broadcasted_iota(jnp.int32, s.shape, 1)
            s = jnp.where(kpos <= qpos, s, -jnp.inf)
        m_new = jnp.maximum(m_old, jnp.max(s, axis=0, keepdims=True))
        alpha = jnp.exp(m_old - m_new)
        p = jnp.exp(s - m_new)
        l_new = alpha * l_old + jnp.sum(p, axis=0, keepdims=True)
        acc_scr[...] = alpha * acc_scr[...] + jnp.dot(vt_scr[:, pl.ds(r0, tkc)], p.astype(BF16),
                                                      preferred_element_type=F32)
        return m_new, l_new

    acc_scr[...] = jnp.zeros_like(acc_scr)
    carry = (jnp.full((1, tq), -jnp.inf, F32), jnp.zeros((1, tq), F32))
    n_full = (past + qi * tq + 1) // tkc
    n_need = (past + qi * tq + tq - 1) // tkc + 1
    n_pair = n_full // 2
    carry = lax.fori_loop(0, n_pair, lambda i, cr: chunk(2 * i + 1, chunk(2 * i, cr, False), False), carry)
    carry = lax.fori_loop(2 * n_pair, n_full, lambda c, cr: chunk(c, cr, False), carry)
    m_fin, l_fin = lax.fori_loop(n_full, n_need, lambda c, cr: chunk(c, cr, True), carry)
    o_ref[...] = (acc_scr[...] / l_fin).T.astype(o_ref.dtype)


def fox_attention(q_src, q_col, k_src, k_col, v_src, v_col, F_all, n_heads, T):
    B = q_src.shape[0]
    Tk = k_src.shape[1]
    past = Tk - T
    tq = min(T, 512)
    tkc = 256 if Tk % 256 == 0 else Tk
    assert T % tq == 0
    f_heads = F_all.transpose(0, 2, 1)
    f_row = f_heads[:, :, None, past:]
    f_rep = jnp.broadcast_to(f_heads[..., None], (B, n_heads, Tk, LANES))

    def kv(col0):
        return pl.BlockSpec((None, Tk, DH_D), lambda b, h, i: (b, 0, col0 + h))

    return pl.pallas_call(
        functools.partial(_fox_kernel, tq=tq, tkc=tkc, past=past),
        grid=(B, n_heads, T // tq),
        in_specs=[pl.BlockSpec((None, tq, DH_D), lambda b, h, i: (b, i, q_col + h)),
                  kv(k_col), kv(v_col),
                  pl.BlockSpec((None, None, 1, tq), lambda b, h, i: (b, h, 0, i)),
                  pl.BlockSpec((None, None, Tk, LANES), lambda b, h, i: (b, h, 0, 0))],
        out_specs=pl.BlockSpec((None, tq, DH_D), lambda b, h, i: (b, i, h)),
        out_shape=jax.ShapeDtypeStruct((B, T, n_heads * DH_D), BF16),
        scratch_shapes=[pltpu.VMEM((Tk, DH_D), BF16), pltpu.VMEM((DH_D, Tk), BF16), pltpu.VMEM((DH_D, tq), F32)],
        compiler_params=_params(("parallel", "parallel", "arbitrary"), 40),
        name="fox_attention",
    )(q_src, k_src, v_src, f_row, f_rep)


SUBLANES = 8


def _bitonic_merge_desc(v):
    v = list(v)
    n = len(v)
    step = n // 2
    while step >= 1:
        for i in range(n):
            if i & step == 0:
                a, b = v[i], v[i + step]
                v[i], v[i + step] = jnp.maximum(a, b), jnp.minimum(a, b)
        step //= 2
    return v


def _bitonic_sort_desc(v):
    v = list(v)
    n = len(v)
    size = 2
    while size <= n:
        step = size // 2
        while step >= 1:
            for i in range(n):
                if i & step == 0:
                    a, b = v[i], v[i + step]
                    hi, lo = jnp.maximum(a, b), jnp.minimum(a, b)
                    v[i], v[i + step] = (hi, lo) if i & size == 0 else (lo, hi)
            step //= 2
        size *= 2
    return v


def _merge_top(a, b):
    n = len(a)
    c = [jnp.maximum(a[i], b[n - 1 - i]) if n - 1 - i < len(b) else a[i] for i in range(n)]
    return _bitonic_merge_desc(c)


def _top_of_keys(s):
    slots = [s[SUBLANES * j:SUBLANES * (j + 1), :] for j in range(s.shape[0] // SUBLANES)]
    assert len(slots) == TOPK_P
    v = _bitonic_sort_desc(slots)
    shift = SUBLANES // 2
    while shift >= 1:
        partner = [pltpu.roll(x, shift, axis=0) for x in v]
        v = _merge_top(v, partner)
        shift //= 2
    return v


def _peer_score_kernel(h_ref, wq_ref, sk_ref, s1_ref, s2_ref, tau_ref, mz_ref, *, n_heads, n_keys):
    qT = lax.dot_general(wq_ref[...], h_ref[...], (((1,), (1,)), ((), ())), preferred_element_type=F32)
    dq = qT.shape[0] // (2 * n_heads)
    t = qT.shape[1]
    head_row = lax.broadcasted_iota(jnp.int32, (SUBLANES, t), 0)
    tops = [[jnp.zeros((SUBLANES, t), F32)] * TOPK_P for _ in range(2)]
    for hd in range(n_heads):
        for c in range(2):
            hc = 2 * hd + c
            q = qT[hc * dq:(hc + 1) * dq, :].astype(BF16)
            s = jnp.dot(sk_ref[hc], q, preferred_element_type=F32)
            if c == 0:
                s1_ref[:, hd, :] = s
            else:
                s2_ref[hd] = s
            top = _top_of_keys(s)
            tops[c] = [jnp.where(head_row == hd, top[i], tops[c][i]) for i in range(TOPK_P)]
    v1, v2 = tops
    best = [v1[0] + v2[b] for b in range(TOPK_P)]
    for a in range(1, TOPK_P):
        best = _merge_top(best, [v1[a] + v2[b] for b in range(TOPK_P // (a + 1))])
    z = jnp.zeros_like(best[0])
    for s_k in best:
        z = z + jnp.exp(s_k - best[0])
    tau_ref[...] = best[TOPK_P - 1]
    mz_ref[...] = best[0] + jnp.log(z)


def peer_scores(h2d, wq_t, subkeys_b):
    N, d = h2d.shape
    n_hc, n_keys, dq = subkeys_b.shape
    n_heads = n_hc // 2
    assert n_heads == SUBLANES and n_keys == TOPK_P * SUBLANES
    tt = min(N, 256)
    return pl.pallas_call(
        functools.partial(_peer_score_kernel, n_heads=n_heads, n_keys=n_keys),
        grid=(N // tt,),
        in_specs=[pl.BlockSpec((tt, d), lambda i: (i, 0)),
                  pl.BlockSpec(wq_t.shape, lambda i: (0, 0), pipeline_mode=pl.Buffered(1)),
                  pl.BlockSpec(subkeys_b.shape, lambda i: (0, 0, 0))],
        out_specs=[pl.BlockSpec((n_keys, n_heads, tt), lambda i: (0, 0, i)),
                   pl.BlockSpec((n_heads, n_keys, tt), lambda i: (0, 0, i)),
                   pl.BlockSpec((n_heads, tt), lambda i: (0, i)),
                   pl.BlockSpec((n_heads, tt), lambda i: (0, i))],
        out_shape=[jax.ShapeDtypeStruct((n_keys, n_heads, N), F32),
                   jax.ShapeDtypeStruct((n_heads, n_keys, N), F32),
                   jax.ShapeDtypeStruct((n_heads, N), F32),
                   jax.ShapeDtypeStruct((n_heads, N), F32)],
        compiler_params=_params(("parallel",), 48),
        name="peer_scores",
    )(h2d, wq_t, subkeys_b)


def _peer_expert_kernel(h_ref, u_ref, vt_ref, s1_ref, s2_ref, tau_ref, mz_ref, o_ref, w0_scr, w1_scr, g0_scr, g1_scr,
                        *, n_heads, n_keys, rows, n_tiles):
    j = pl.program_id(1)
    tt = h_ref.shape[0]
    d = o_ref.shape[0]
    cw = min(tt, LANES)
    w_scr = (w0_scr, w1_scr)
    g_scr = (g0_scr, g1_scr)
    gate_pieces = [(r, c0) for r in range(rows) for c0 in range(0, tt, cw)]
    n_g = len(gate_pieces)

    def gate_piece(tile, g_ref, r, c0):
        cols = slice(c0, c0 + cw)
        s1_heads = s1_ref[tile * rows + r]
        g = jnp.zeros((n_keys, cw), F32)
        for hd in range(n_heads):
            s = s1_heads[hd:hd + 1, cols] + s2_ref[hd, :, cols]
            e = jnp.exp(s - mz_ref[hd:hd + 1, cols])
            g = g + jnp.where(s >= tau_ref[hd:hd + 1, cols], e, 0.0)
        g_ref[r * n_keys:(r + 1) * n_keys, cols] = g

    def score_piece(g_ref, w_ref, piece, n_p):
        cols = slice(piece * (tt // n_p), (piece + 1) * (tt // n_p))
        hT = lax.dot_general(u_ref[...], h_ref[cols, :], (((1,), (1,)), ((), ())), preferred_element_type=F32)
        w_ref[:, cols] = (jax.nn.gelu(hT) * g_ref[:, cols]).astype(BF16)

    def accumulate_piece(w_ref, m, n_m):
        rws = slice(m * (d // n_m), (m + 1) * (d // n_m))
        o_ref[rws, :] += jnp.dot(vt_ref[rws, :], w_ref[...], preferred_element_type=F32)

    n_sp = 2 if tt >= 2 * LANES else 1

    @pl.when(j == 0)
    def _():
        o_ref[...] = jnp.zeros_like(o_ref)
        for r, c0 in gate_pieces:
            gate_piece(j, g_scr[0], r, c0)
        for p in range(n_sp):
            score_piece(g_scr[0], w_scr[0], p, n_sp)
        for r, c0 in gate_pieces:
            gate_piece(j + 1, g_scr[1], r, c0)

    for par in range(2):
        @pl.when(jnp.logical_and(jnp.logical_and(j > 0, j < n_tiles), j % 2 == par))
        def _():
            nxt = jnp.minimum(j + 1, n_tiles - 1)
            for p in range(n_sp):
                score_piece(g_scr[par], w_scr[par], p, n_sp)
            for m in range(n_g):
                accumulate_piece(w_scr[1 - par], m, n_g)
                gate_piece(nxt, g_scr[1 - par], *gate_pieces[m])

    @pl.when(j == n_tiles)
    def _():
        accumulate_piece(w_scr[(n_tiles - 1) % 2], 0, 1)


def peer_experts(h2d, u_all, vt_all, layer, s1, s2, tau, mz):
    N, d = h2d.shape
    n_exp = u_all.shape[1]
    n_heads, n_keys, _ = s2.shape
    tt = min(N, 512)
    te = 512
    rows = te // n_keys
    n_tiles = n_exp // te
    assert n_tiles % 2 == 0
    once = pl.Buffered(1)
    return pl.pallas_call(
        functools.partial(_peer_expert_kernel, n_heads=n_heads, n_keys=n_keys, rows=rows, n_tiles=n_tiles),
        grid=(N // tt, n_tiles + 1),
        in_specs=[pl.BlockSpec((tt, d), lambda i, j: (i, 0), pipeline_mode=once),
                  pl.BlockSpec((None, te, d), lambda i, j: (layer, jnp.minimum(j, n_tiles - 1), 0)),
                  pl.BlockSpec((None, d, te), lambda i, j: (layer, 0, jnp.maximum(j - 1, 0))),
                  pl.BlockSpec((n_keys, n_heads, tt), lambda i, j: (0, 0, i), pipeline_mode=once),
                  pl.BlockSpec((n_heads, n_keys, tt), lambda i, j: (0, 0, i), pipeline_mode=once),
                  pl.BlockSpec((n_heads, tt), lambda i, j: (0, i), pipeline_mode=once),
                  pl.BlockSpec((n_heads, tt), lambda i, j: (0, i), pipeline_mode=once)],
        out_specs=pl.BlockSpec((d, tt), lambda i, j: (0, i)),
        out_shape=jax.ShapeDtypeStruct((d, N), F32),
        scratch_shapes=[pltpu.VMEM((te, tt), BF16)] * 2 + [pltpu.VMEM((te, tt), F32)] * 2,
        compiler_params=_params(("parallel", "arbitrary"), 56),
        name="peer_experts",
    )(h2d, u_all, vt_all, s1, s2, tau, mz)


def peer_ffn(h, wq_t, subkeys_b, u_all, vt_all, layer):
    B, T, d = h.shape
    h2d = h.reshape(B * T, d)
    s1, s2, tau, mz = peer_scores(h2d, wq_t, subkeys_b)
    return peer_experts(h2d, u_all, vt_all, layer, s1, s2, tau, mz)


def _pad_cols(w, n):
    return jnp.pad(w, ((0, 0), (0, n - w.shape[1])))


def kernel(x_prompt, x_sample, c_prompt, c_sample, state_rwkv_shift, state_rwkv, state_ret, cache_fox_k, cache_fox_v, cache_fox_logf, ada_w, ada_b, norm_g, final_g, w_in_even, w_out_even, sgu_norm_g, sgu_w, sgu_b, rwkv_mu, rwkv_w0, rwkv_w_up, rwkv_a0, rwkv_a_up, rwkv_g_up, rwkv_k_k, rwkv_k_a, rwkv_r_k, rwkv_ln_w, rwkv_ln_b, w_in_odd, w_out_odd, ret_gn_g, fox_b_f, peer_wq, peer_subkeys, peer_u, peer_v):
    depth = ada_w.shape[0]
    d = x_prompt.shape[-1]
    d_a = d_b = d // 2
    h_b = d_b // DH_B
    h_c = d // (2 * DV_C)
    h_d = d // (2 * DH_D)
    p_b = rwkv_mu.shape[1]
    n_bp, n_bs = x_prompt.shape[0], x_sample.shape[0]
    past_len = cache_fox_k.shape[2]

    n_rows = _round_up(n_bp + n_bs, 16)
    c_all = jnp.pad(jnp.concatenate([c_prompt, c_sample], axis=0), ((0, n_rows - n_bp - n_bs), (0, 0)))
    mod4 = ada_modulation(c_all, ada_w, ada_b).reshape(depth, n_rows, 1, 6 * d)

    w_in_e = [_pad_cols(w_in_even[j], 2 * d_a + 3 * d_b + TAIL_BLOCK).astype(BF16) for j in range(w_in_even.shape[0])]
    w_in_o = [_pad_cols(w_in_odd[j], _round_up(w_in_odd.shape[2], 512)).astype(BF16) for j in range(w_in_odd.shape[0])]
    w_out_e = [w_out_even[j].astype(BF16) for j in range(w_out_even.shape[0])]
    w_out_o = [w_out_odd[j].astype(BF16) for j in range(w_out_odd.shape[0])]
    n_hp = peer_subkeys.shape[1]
    wq_t = [peer_wq[i].T.astype(BF16) for i in range(depth)]
    subk = [peer_subkeys[i].reshape(2 * n_hp, peer_subkeys.shape[3], peer_subkeys.shape[4]).astype(BF16)
            for i in range(depth)]
    u_all = peer_u.astype(BF16)
    vt_all = jnp.swapaxes(peer_v, 1, 2).astype(BF16)

    def run(x, row0, pos0, shift0, rwkv0, ret0, fox_past):
        B, T, _ = x.shape
        pos = pos0 + jnp.arange(T)
        v_rows, shifts, rwkvs, rets, ks, vs, lfs = [], [], [], [], [], [], []
        pending = None
        for i in range(depth):
            j = i // 2
            res = None if pending is None else (pending[0], pending[1], mod4, pending[3], row0, pending[2])
            x_new, h = resnorm(x, norm_g[i, 0], res=res, mod=(mod4, i, row0, 1, 0))
            x = x if x_new is None else x_new
            h2d = h.reshape(B * T, d)
            if i % 2 == 0:
                p = matmul(h2d, w_in_e[j]).reshape(B, T, -1)
                out_a, vn = spatial_gating(p, d_a, sgu_norm_g[j], sgu_w[j], sgu_b[j])
                r, w, k, v, kk, a, g = rwkv_prep(p, d_b, 2 * d_a, shift0[j], rwkv_mu[j], rwkv_w0[j], rwkv_w_up[j],
                                                 rwkv_a0[j], rwkv_a_up[j], rwkv_g_up[j], rwkv_k_k[j], rwkv_k_a[j])
                yb, S = rwkv_scan(r, w, k, v, kk, a, rwkv0[j], rwkv_r_k[j], rwkv_ln_w[j], rwkv_ln_b[j])
                mix = gate_concat(out_a, yb, g)
                y = matmul(mix.reshape(B * T, d), w_out_e[j]).reshape(B, T, d)
                v_rows.append(vn)
                shifts.append(p[:, -1:, 2 * d_a:2 * d_a + p_b])
                rwkvs.append(S)
            else:
                p = matmul(h2d, w_in_o[j]).reshape(B, T, -1)
                c_dq = 2 * h_c * DK_C + 2 * h_c * DV_C
                c_dk, c_dv, c_df = c_dq + h_d * DH_D, c_dq + 2 * h_d * DH_D, c_dq + 3 * h_d * DH_D
                out_c, R = retention(p, h_c, pos, ret0[j], ret_gn_g[j])
                logf = jax.nn.log_sigmoid(p[..., c_df:c_df + h_d] + fox_b_f[j])
                kd = p[..., c_dk:c_dk + h_d * DH_D]
                vd = p[..., c_dv:c_dv + h_d * DH_D]
                if fox_past is None:
                    F_all = jnp.cumsum(logf, axis=1)
                    od = fox_attention(p, c_dq // DH_D, p, c_dk // DH_D, p, c_dv // DH_D, F_all, h_d, T)
                else:
                    k_past, v_past, lf_past = fox_past[j]
                    k_all = jnp.concatenate([k_past.reshape(B, past_len, -1), kd], axis=1)
                    v_all = jnp.concatenate([v_past.reshape(B, past_len, -1), vd], axis=1)
                    F_all = jnp.cumsum(jnp.concatenate([lf_past, logf], axis=1), axis=1)
                    od = fox_attention(p, c_dq // DH_D, k_all, 0, v_all, 0, F_all, h_d, T)
                mix = jnp.concatenate([out_c, od], axis=-1)
                y = matmul(mix.reshape(B * T, d), w_out_o[j]).reshape(B, T, d)
                rets.append(R)
                ks.append(kd.reshape(B, T, h_d, DH_D))
                vs.append(vd.reshape(B, T, h_d, DH_D))
                lfs.append(logf)
            x, h = resnorm(x, norm_g[i, 1], res=(y, False, mod4, i, row0, 2), mod=(mod4, i, row0, 4, 3))
            y_t = peer_ffn(h, wq_t[i], subk[i], u_all, vt_all, i)
            if T % LANES == 0:
                pending = (y_t, True, 5, i)
            else:
                pending = (y_t.T.reshape(B, T, d), False, 5, i)
        _, out = resnorm(x, final_g, res=(pending[0], pending[1], mod4, pending[3], row0, pending[2]),
                         out_dtype=F32)
        return out, v_rows, shifts, rwkvs, rets, ks, vs, lfs

    n_even, n_odd = w_in_even.shape[0], w_in_odd.shape[0]
    y_p, _, sh_p, rw_p, rt_p, k_p, v_p, lf_p = run(
        x_prompt, 0, 0,
        [jnp.zeros((n_bp, 1, p_b), F32)] * n_even,
        [jnp.zeros((n_bp, h_b, DH_B, DH_B), F32)] * n_even,
        [jnp.zeros((n_bp, h_c, DK_C, DV_C), F32)] * n_odd,
        None)
    y_s, vr_s, sh_s, rw_s, rt_s, k_s, v_s, lf_s = run(
        x_sample, n_bp, past_len,
        [state_rwkv_shift[j] for j in range(n_even)],
        [state_rwkv[j] for j in range(n_even)],
        [state_ret[j] for j in range(n_odd)],
        [(cache_fox_k[j], cache_fox_v[j], cache_fox_logf[j]) for j in range(n_odd)])
    return (y_p, y_s, jnp.stack(vr_s), jnp.stack(sh_p), jnp.stack(sh_s), jnp.stack(rw_p), jnp.stack(rw_s),
            jnp.stack(rt_p), jnp.stack(rt_s), jnp.stack(k_p), jnp.stack(k_s), jnp.stack(v_p), jnp.stack(v_s),
            jnp.stack(lf_p), jnp.stack(lf_s))
```

```python
import functools
import math

import jax
import jax.numpy as jnp
import numpy as np
from jax import lax
from jax.experimental import pallas as pl
from jax.experimental.pallas import tpu as pltpu

F32 = jnp.float32
BF16 = jnp.bfloat16

EPS = 1e-6
RWKV_GN_EPS = 64e-5
ROPE_BASE = 10000.0
RET_CHUNK = 64
MLP_CHUNK = 128
N_GROUPS_A = 8
DH_B = 64
DK_C = 128
DV_C = 256
DH_D = 128
TOPK_P = 16

LANES = 128
TAIL_BLOCK = 512
MIB = 1024 * 1024


def _round_up(n, m):
    return (n + m - 1) // m * m


def _params(sem, vmem_mib):
    return pltpu.CompilerParams(dimension_semantics=sem, vmem_limit_bytes=int(vmem_mib * MIB))


def _mm_kernel(a_ref, b_ref, o_ref):
    o_ref[...] = jnp.dot(a_ref[...], b_ref[...], preferred_element_type=F32).astype(o_ref.dtype)


def matmul(a, b, out_dtype=F32):
    M, K = a.shape
    N = b.shape[1]
    tm = min(M, 1024)
    tn = 512
    assert M % tm == 0 and N % tn == 0, (M, N)
    return pl.pallas_call(
        _mm_kernel,
        grid=(M // tm, N // tn),
        in_specs=[pl.BlockSpec((tm, K), lambda i, j: (i, 0)),
                  pl.BlockSpec((K, tn), lambda i, j: (0, j))],
        out_specs=pl.BlockSpec((tm, tn), lambda i, j: (i, j)),
        out_shape=jax.ShapeDtypeStruct((M, N), out_dtype),
        compiler_params=_params(("parallel", "parallel"), 48),
        name="matmul",
    )(a, b)


def _ada_kernel(c_ref, w_ref, b_ref, o_ref):
    c = c_ref[...]
    a = (c * jax.nn.sigmoid(c)).astype(BF16)
    o_ref[...] = jnp.dot(a, w_ref[...].astype(BF16), preferred_element_type=F32) + b_ref[...]


def ada_modulation(c_all, ada_w, ada_b):
    depth, d, n6 = ada_w.shape
    R = c_all.shape[0]
    tn = 512
    return pl.pallas_call(
        _ada_kernel,
        grid=(depth, n6 // tn),
        in_specs=[pl.BlockSpec((R, d), lambda i, j: (0, 0)),
                  pl.BlockSpec((None, d, tn), lambda i, j: (i, 0, j)),
                  pl.BlockSpec((None, 1, tn), lambda i, j: (i, 0, j))],
        out_specs=pl.BlockSpec((None, R, tn), lambda i, j: (i, 0, j)),
        out_shape=jax.ShapeDtypeStruct((depth, R, n6), F32),
        compiler_params=_params(("parallel", "parallel"), 40),
        name="ada_modulation",
    )(c_all, ada_w, ada_b.reshape(depth, 1, n6))


def _resnorm_kernel(*refs, has_res, y_transposed, has_mod):
    it = iter(refs)
    x_ref = next(it)
    if has_res:
        y_ref = next(it)
        gate_ref = next(it)
    g_ref = next(it)
    if has_mod:
        sc_ref = next(it)
        sh_ref = next(it)
    if has_res:
        xo_ref = next(it)
    h_ref = next(it)
    x = x_ref[...]
    if has_res:
        y = y_ref[...]
        if y_transposed:
            y = y.T
        x = x + gate_ref[...] * y
        xo_ref[...] = x
    xn = x * lax.rsqrt(jnp.mean(x * x, axis=-1, keepdims=True) + EPS) * g_ref[...]
    if has_mod:
        xn = xn * (1.0 + sc_ref[...]) + sh_ref[...]
    h_ref[...] = xn.astype(h_ref.dtype)


def resnorm(x, norm_g, *, res=None, mod=None, out_dtype=BF16):
    B, T, d = x.shape
    tt = min(T, 256)
    nt = T // tt
    row = pl.BlockSpec((None, tt, d), lambda b, i: (b, i, 0))
    args, specs = [x], [row]
    if res is not None:
        y, y_transposed, mod4, layer, row0, gk = res
        args.append(y)
        if y_transposed:
            specs.append(pl.BlockSpec((d, tt), lambda b, i: (0, b * nt + i)))
        else:
            specs.append(row)
        args.append(mod4)
        specs.append(pl.BlockSpec((None, None, 1, d), lambda b, i: (layer, row0 + b, 0, gk)))
    else:
        y_transposed = False
    args.append(norm_g.reshape(1, d))
    specs.append(pl.BlockSpec((1, d), lambda b, i: (0, 0)))
    if mod is not None:
        mod4m, layer_m, row0_m, sck, shk = mod
        args += [mod4m, mod4m]
        specs.append(pl.BlockSpec((None, None, 1, d), lambda b, i: (layer_m, row0_m + b, 0, sck)))
        specs.append(pl.BlockSpec((None, None, 1, d), lambda b, i: (layer_m, row0_m + b, 0, shk)))
    out_shape, out_specs = [], []
    if res is not None:
        out_shape.append(jax.ShapeDtypeStruct((B, T, d), F32))
        out_specs.append(row)
    out_shape.append(jax.ShapeDtypeStruct((B, T, d), out_dtype))
    out_specs.append(row)
    outs = pl.pallas_call(
        functools.partial(_resnorm_kernel, has_res=res is not None, y_transposed=y_transposed,
                          has_mod=mod is not None),
        grid=(B, nt),
        in_specs=specs,
        out_specs=out_specs,
        out_shape=out_shape,
        compiler_params=_params(("parallel", "parallel"), 48),
        name="resnorm",
    )(*args)
    if res is not None:
        return outs[0], outs[1]
    return None, outs[0]


def _sgu_kernel(pu_ref, pv_ref, ng_ref, w_ref, b_ref, oa_ref, vn_ref, *, groups):
    u = jax.nn.gelu(pu_ref[...])
    v = jax.nn.gelu(pv_ref[...])
    dg = v.shape[1] // groups
    for g in range(groups):
        sl = slice(g * dg, (g + 1) * dg)
        vg = v[:, sl]
        mean = jnp.mean(vg, axis=-1, keepdims=True)
        cen = vg - mean
        var = jnp.mean(cen * cen, axis=-1, keepdims=True)
        vn = cen * lax.rsqrt(var + EPS) * ng_ref[:, sl]
        vn_ref[:, sl] = vn
        s = jnp.dot(w_ref[g], vn.astype(BF16), preferred_element_type=F32) + b_ref[:, sl]
        oa_ref[:, sl] = (u[:, sl] * s).astype(oa_ref.dtype)


def spatial_gating(p, d_a, norm_g, w_s, b_s):
    B, T, _ = p.shape
    L = min(T, MLP_CHUNK)
    G = w_s.shape[0]
    dg = d_a // G
    w_tril = jnp.tril(w_s)[:, :L, :L].astype(BF16)
    b_full = jnp.repeat(b_s.T[:L], dg, axis=1)
    blk = lambda c: pl.BlockSpec((None, L, d_a), lambda b, i: (b, i, c))
    return pl.pallas_call(
        functools.partial(_sgu_kernel, groups=G),
        grid=(B, T // L),
        in_specs=[blk(0), blk(1),
                  pl.BlockSpec((1, d_a), lambda b, i: (0, 0)),
                  pl.BlockSpec((G, L, L), lambda b, i: (0, 0, 0)),
                  pl.BlockSpec((L, d_a), lambda b, i: (0, 0))],
        out_specs=[blk(0), blk(0)],
        out_shape=[jax.ShapeDtypeStruct((B, T, d_a), BF16), jax.ShapeDtypeStruct((B, T, d_a), F32)],
        compiler_params=_params(("parallel", "parallel"), 32),
        name="spatial_gating",
    )(p, p, norm_g.reshape(1, d_a), w_tril, b_full)


def _shifted(x_ref, halo_ref, state_ref, mu_ref):
    x = x_ref[...]
    first = jnp.where(pl.program_id(1) == 0, state_ref[...], halo_ref[7:8, :])
    rows = lax.broadcasted_iota(jnp.int32, x.shape, 0)
    prev = jnp.where(rows == 0, first, pltpu.roll(x, 1, axis=0))
    return x + (prev - x) * mu_ref[...]


def _rwkv_prep_kernel(r_ref, k_ref, v_ref, t_ref, rh_ref, kh_ref, vh_ref, th_ref,
                      sr_ref, sk_ref, sv_ref, st_ref, mr_ref, mk_ref, mv_ref, mt_ref,
                      w0_ref, wup_ref, a0_ref, aup_ref, gup_ref,
                      ro_ref, wo_ref, ko_ref, vo_ref, ao_ref, go_ref, *, lw, la, lg):
    r = _shifted(r_ref, rh_ref, sr_ref, mr_ref)
    k = _shifted(k_ref, kh_ref, sk_ref, mk_ref)
    v = _shifted(v_ref, vh_ref, sv_ref, mv_ref)
    t = _shifted(t_ref, th_ref, st_ref, mt_ref)
    wd, ad, gd = t[:, :lw], t[:, lw:lw + la], t[:, lw + la:lw + la + lg]
    wl = w0_ref[...] + jnp.dot(jnp.tanh(wd).astype(BF16), wup_ref[...], preferred_element_type=F32)
    w_log = -jax.nn.softplus(-wl) - 0.5
    wo_ref[...] = jnp.exp(-jnp.exp(w_log))
    a = jax.nn.sigmoid(a0_ref[...] + jnp.dot(ad.astype(BF16), aup_ref[...], preferred_element_type=F32))
    go_ref[...] = jnp.dot(jax.nn.sigmoid(gd).astype(BF16), gup_ref[...], preferred_element_type=F32)
    ro_ref[...] = r
    vo_ref[...] = v
    ao_ref[...] = a
    ko_ref[...] = k


def rwkv_prep(p, d_b, col0, shift_state, mu, w0, w_up, a0, a_up, g_up):
    B, T, _ = p.shape
    lw, la, lg = w_up.shape[0], a_up.shape[0], g_up.shape[0]
    tt = min(T, 128)
    cb = col0 // d_b
    tb = (col0 + 3 * d_b) // TAIL_BLOCK
    assert col0 % d_b == 0 and (col0 + 3 * d_b) % TAIL_BLOCK == 0 and lw + la + lg <= TAIL_BLOCK
    pad = TAIL_BLOCK - (lw + la + lg)

    def seg(c, width):
        return pl.BlockSpec((None, tt, width), lambda b, i: (b, i, c))

    def halo(c, width):
        return pl.BlockSpec((None, 8, width), lambda b, i: (b, jnp.maximum(i * (tt // 8) - 1, 0), c))

    def per_b(width):
        return pl.BlockSpec((None, 1, width), lambda b, i: (b, 0, 0))

    def full(shape):
        return pl.BlockSpec(shape, lambda b, i: tuple(0 for _ in shape))

    st = shift_state
    s_parts = [st[..., :d_b], st[..., d_b:2 * d_b], st[..., 2 * d_b:3 * d_b],
               jnp.pad(st[..., 3 * d_b:], ((0, 0), (0, 0), (0, pad)))]
    m_parts = [mu[None, :d_b], mu[None, d_b:2 * d_b], mu[None, 2 * d_b:3 * d_b],
               jnp.pad(mu[None, 3 * d_b:], ((0, 0), (0, pad)))]
    out_row = pl.BlockSpec((None, tt, d_b), lambda b, i: (b, i, 0))
    outs = pl.pallas_call(
        functools.partial(_rwkv_prep_kernel, lw=lw, la=la, lg=lg),
        grid=(B, T // tt),
        in_specs=[seg(cb, d_b), seg(cb + 1, d_b), seg(cb + 2, d_b), seg(tb, TAIL_BLOCK),
                  halo(cb, d_b), halo(cb + 1, d_b), halo(cb + 2, d_b), halo(tb, TAIL_BLOCK),
                  per_b(d_b), per_b(d_b), per_b(d_b), per_b(TAIL_BLOCK),
                  full((1, d_b)), full((1, d_b)), full((1, d_b)), full((1, TAIL_BLOCK)),
                  full((1, d_b)), full((lw, d_b)), full((1, d_b)), full((la, d_b)), full((lg, d_b))],
        out_specs=[out_row] * 6,
        out_shape=[jax.ShapeDtypeStruct((B, T, d_b), F32)] * 6,
        compiler_params=_params(("parallel", "arbitrary"), 56),
        name="rwkv_prep",
    )(p, p, p, p, p, p, p, p, *s_parts, *m_parts,
      w0[None], w_up.astype(BF16), a0[None], a_up.astype(BF16), g_up.astype(BF16))
    return outs


def _rwkv_scan_kernel(r_ref, w_ref, k_ref, v_ref, a_ref, kk_ref, ka_ref, rk_ref, lnw_ref, lnb_ref, s0_ref,
                      y_ref, so_ref, s_scr, av_scr, bv_scr, k2_scr, *, tc, dh):
    @pl.when(pl.program_id(1) == 0)
    def _():
        s_scr[...] = s0_ref[...]

    def step(t, carry):
        a = a_ref[t]
        kr = k_ref[t] * kk_ref[...]
        kk = kr * lax.rsqrt(jnp.maximum(jnp.sum(kr * kr, axis=0, keepdims=True), 1e-24))
        av_scr[...] = -kk
        bv_scr[...] = kk * a
        k2 = k_ref[t] * (1.0 + (a - 1.0) * ka_ref[...])
        k2_scr[...] = k2
        v = v_ref[t]
        sa = jnp.zeros_like(v)
        for k in range(dh):
            sa = sa + s_scr[k] * av_scr[k:k + 1, :]
        y = jnp.zeros_like(v)
        for k in range(dh):
            s_new = (s_scr[k] * w_ref[t, k:k + 1, :] + sa * bv_scr[k:k + 1, :] + v * k2_scr[k:k + 1, :])
            s_scr[k] = s_new
            y = y + s_new * r_ref[t, k:k + 1, :]
        mean = jnp.mean(y, axis=0, keepdims=True)
        cen = y - mean
        var = jnp.mean(cen * cen, axis=0, keepdims=True)
        yn = cen * lax.rsqrt(var + RWKV_GN_EPS) * lnw_ref[...] + lnb_ref[...]
        bonus = jnp.sum(r_ref[t] * k2 * rk_ref[...], axis=0, keepdims=True) * v
        y_ref[t] = yn + bonus
        return carry

    lax.fori_loop(0, tc, step, 0)

    @pl.when(pl.program_id(1) == pl.num_programs(1) - 1)
    def _():
        so_ref[...] = s_scr[...]


def rwkv_scan(r, w, k, v, a, S0, k_k, k_a, r_k, ln_w, ln_b):
    B, T, d_b = r.shape
    H = d_b // DH_B
    BH = B * H
    BHp = _round_up(BH, LANES)

    def to_lanes(x):
        x = x.reshape(B, T, H, DH_B).transpose(1, 3, 0, 2).reshape(T, DH_B, BH)
        return jnp.pad(x, ((0, 0), (0, 0), (0, BHp - BH)))

    def param_lanes(x):
        x = jnp.tile(x.T, (1, B))
        return jnp.pad(x, ((0, 0), (0, BHp - BH)))

    s0 = S0.transpose(3, 2, 0, 1).reshape(DH_B, DH_B, BH)
    s0 = jnp.pad(s0, ((0, 0), (0, 0), (0, BHp - BH)))
    tc = min(T, 32)
    seq = pl.BlockSpec((tc, DH_B, LANES), lambda l, i: (i, 0, l))
    par = pl.BlockSpec((DH_B, LANES), lambda l, i: (0, l))
    st = pl.BlockSpec((DH_B, DH_B, LANES), lambda l, i: (0, 0, l))
    y, s_fin = pl.pallas_call(
        functools.partial(_rwkv_scan_kernel, tc=tc, dh=DH_B),
        grid=(BHp // LANES, T // tc),
        in_specs=[seq] * 5 + [par] * 5 + [st],
        out_specs=[seq, st],
        out_shape=[jax.ShapeDtypeStruct((T, DH_B, BHp), F32), jax.ShapeDtypeStruct((DH_B, DH_B, BHp), F32)],
        scratch_shapes=[pltpu.VMEM((DH_B, DH_B, LANES), F32)] + [pltpu.VMEM((DH_B, LANES), F32)] * 3,
        compiler_params=_params(("parallel", "arbitrary"), 40),
        name="rwkv_scan",
    )(to_lanes(r), to_lanes(w), to_lanes(k), to_lanes(v), to_lanes(a),
      param_lanes(k_k.reshape(H, DH_B)), param_lanes(k_a.reshape(H, DH_B)), param_lanes(r_k),
      param_lanes(ln_w.reshape(H, DH_B)), param_lanes(ln_b.reshape(H, DH_B)), s0)
    y = y[:, :, :BH].reshape(T, DH_B, B, H).transpose(2, 0, 3, 1).reshape(B, T, d_b)
    s_fin = s_fin[:, :, :BH].reshape(DH_B, DH_B, B, H).transpose(2, 3, 1, 0)
    return y, s_fin


def _gate_concat_kernel(oa_ref, yb_ref, g_ref, o_ref, *, d_a):
    o_ref[:, :d_a] = oa_ref[...]
    o_ref[:, d_a:] = (yb_ref[...] * g_ref[...]).astype(o_ref.dtype)


def gate_concat(out_a, yb, g):
    B, T, d_a = out_a.shape
    d_b = yb.shape[-1]
    tt = min(T, 256)
    blk = lambda w: pl.BlockSpec((None, tt, w), lambda b, i: (b, i, 0))
    return pl.pallas_call(
        functools.partial(_gate_concat_kernel, d_a=d_a),
        grid=(B, T // tt),
        in_specs=[blk(d_a), blk(d_b), blk(d_b)],
        out_specs=blk(d_a + d_b),
        out_shape=jax.ShapeDtypeStruct((B, T, d_a + d_b), BF16),
        compiler_params=_params(("parallel", "parallel"), 32),
        name="gate_concat",
    )(out_a, yb, g)


def _rope(x, cos2, sin2):
    return x * cos2 + pltpu.roll(x, x.shape[-1] // 2, axis=1) * sin2


def _retention_kernel(q_ref, k_ref, v_ref, g_ref, cos_ref, sin_ref, dm_ref, kd_ref, qd_ref, gl_ref, gn_ref,
                      r0_ref, o_ref, ro_ref, r_scr, *, L, n_chunks):
    @pl.when(pl.program_id(2) == 0)
    def _():
        r_scr[...] = r0_ref[...]

    scale = DK_C ** -0.5
    for c in range(n_chunks):
        rows = slice(c * L, (c + 1) * L)
        cos2, sin2 = cos_ref[rows, :], sin_ref[rows, :]
        q = _rope(q_ref[rows, :], cos2, sin2)
        k = _rope(k_ref[rows, :], cos2, sin2) * scale
        v = v_ref[rows, :].astype(BF16)
        att = lax.dot_general(q.astype(BF16), k.astype(BF16), (((1,), (1,)), ((), ())),
                              preferred_element_type=F32) * dm_ref[...]
        R = r_scr[...]
        o = jnp.dot(att.astype(BF16), v, preferred_element_type=F32)
        o = o + jnp.dot((q * qd_ref[...]).astype(BF16), R.astype(BF16), preferred_element_type=F32)
        kd_t = (k * kd_ref[...]).T.astype(BF16)
        r_scr[...] = R * gl_ref[...] + jnp.dot(kd_t, v, preferred_element_type=F32)
        mean = jnp.mean(o, axis=-1, keepdims=True)
        cen = o - mean
        var = jnp.mean(cen * cen, axis=-1, keepdims=True)
        on = cen * lax.rsqrt(var + EPS) * gn_ref[...]
        gate = g_ref[rows, :]
        o_ref[rows, :] = (gate * jax.nn.sigmoid(gate) * on).astype(o_ref.dtype)

    @pl.when(pl.program_id(2) == pl.num_programs(2) - 1)
    def _():
        ro_ref[...] = r_scr[...]


def retention(p, h_c, pos, R0, gn_g):
    B, T, _ = p.shape
    L = math.gcd(T, RET_CHUNK)
    tr = min(T, 512)
    n_chunks = tr // L
    half = DK_C // 2
    inv = ROPE_BASE ** (-jnp.arange(half, dtype=F32) / half)
    ang = pos.astype(F32)[:, None] * inv[None, :]
    cos2 = jnp.concatenate([jnp.cos(ang), jnp.cos(ang)], axis=-1)
    sin2 = jnp.concatenate([-jnp.sin(ang), jnp.sin(ang)], axis=-1)
    log_g = jnp.log1p(-jnp.exp2(-5.0 - jnp.arange(h_c, dtype=F32)))
    idx = jnp.arange(L, dtype=F32)
    rel = idx[:, None] - idx[None, :]
    dmat = jnp.where(rel >= 0, jnp.exp(jnp.maximum(rel, 0.0) * log_g[:, None, None]), 0.0)
    k_dec = jnp.exp((L - 1 - idx)[None, :, None] * log_g[:, None, None])
    q_dec = jnp.exp((idx + 1)[None, :, None] * log_g[:, None, None])
    k_dec = jnp.broadcast_to(k_dec, (h_c, L, DK_C))
    q_dec = jnp.broadcast_to(q_dec, (h_c, L, DK_C))
    g_l = jnp.broadcast_to(jnp.exp(L * log_g)[:, None, None], (h_c, 1, DV_C))

    def col(width, c0):
        return pl.BlockSpec((None, tr, width), lambda b, h, i: (b, i, c0 + h))

    def head(shape):
        return pl.BlockSpec((None,) + shape, lambda b, h, i: (h,) + tuple(0 for _ in shape))

    trig = pl.BlockSpec((tr, DK_C), lambda b, h, i: (i, 0))
    state = pl.BlockSpec((None, None, DK_C, DV_C), lambda b, h, i: (b, h, 0, 0))
    return pl.pallas_call(
        functools.partial(_retention_kernel, L=L, n_chunks=n_chunks),
        grid=(B, h_c, T // tr),
        in_specs=[col(DK_C, 0), col(DK_C, h_c), col(DV_C, h_c), col(DV_C, 2 * h_c), trig, trig,
                  head((L, L)), head((L, DK_C)), head((L, DK_C)), head((1, DV_C)),
                  pl.BlockSpec((1, DV_C), lambda b, h, i: (0, h)), state],
        out_specs=[pl.BlockSpec((None, tr, DV_C), lambda b, h, i: (b, i, h)), state],
        out_shape=[jax.ShapeDtypeStruct((B, T, h_c * DV_C), BF16),
                   jax.ShapeDtypeStruct((B, h_c, DK_C, DV_C), F32)],
        scratch_shapes=[pltpu.VMEM((DK_C, DV_C), F32)],
        compiler_params=_params(("parallel", "parallel", "arbitrary"), 32),
        name="retention",
    )(p, p, p, p, cos2, sin2, dmat, k_dec, q_dec, g_l, gn_g.reshape(1, h_c * DV_C), R0)


def _fox_kernel(q_ref, k_ref, v_ref, fq_ref, fk_ref, o_ref, kb_scr, vt_scr, acc_scr, *, tq, tkc, past):
    qi = pl.program_id(2)
    n_kv = k_ref.shape[0] // tkc

    @pl.when(qi == 0)
    def _():
        for c in range(n_kv):
            rows = slice(c * tkc, (c + 1) * tkc)
            kb_scr[rows, :] = k_ref[rows, :].astype(BF16)
            vt_scr[:, rows] = v_ref[rows, :].T.astype(BF16)

    q = q_ref[...].astype(BF16)
    fq = fq_ref[...]
    scale = DH_D ** -0.5
    n_rep = max(tq // LANES, 1)

    def scores(c, masked):
        r0 = 0 if n_kv == 1 else pl.multiple_of(c * tkc, tkc)
        s = lax.dot_general(kb_scr[pl.ds(r0, tkc), :], q, (((1,), (1,)), ((), ())), preferred_element_type=F32)
        fk = fk_ref[pl.ds(r0, tkc), :]
        fk = jnp.concatenate([fk] * n_rep, axis=1) if tq >= LANES else fk[:, :tq]
        s = s * scale + (fq - fk)
        if masked:
            kpos = r0 + lax.broadcasted_iota(jnp.int32, s.shape, 0)
            qpos = past + qi * tq + lax.broadcasted_iota(jnp.int32, s.shape, 1)
            s = jnp.where(kpos <= qpos, s, -jnp.inf)
        return r0, s

    def chunks(cs, carry, masked):
        m_old, l_old = carry
        parts = [scores(c, masked) for c in cs]
        m_new = m_old
        for _, s in parts:
            m_new = jnp.maximum(m_new, jnp.max(s, axis=0, keepdims=True))
        alpha = jnp.exp(m_old - m_new)
        l_new = alpha * l_old
        upd = alpha * acc_scr[...]
        for r0, s in parts:
            p = jnp.exp(s - m_new)
            l_new = l_new + jnp.sum(p, axis=0, keepdims=True)
            upd = upd + jnp.dot(vt_scr[:, pl.ds(r0, tkc)], p.astype(BF16), preferred_element_type=F32)
        acc_scr[...] = upd
        return m_new, l_new

    def sweep(lo, hi, carry, masked):
        for width in (4, 2, 1):
            n_grp = (hi - lo) // width
            carry = lax.fori_loop(
                0, n_grp, lambda i, cr, lo=lo, width=width: chunks(tuple(lo + width * i + u for u in range(width)),
                                                                    cr, masked), carry)
            lo = lo + width * n_grp
        return carry

    acc_scr[...] = jnp.zeros_like(acc_scr)
    carry = (jnp.full((1, tq), -jnp.inf, F32), jnp.zeros((1, tq), F32))
    n_full = (past + qi * tq + 1) // tkc
    n_need = (past + qi * tq + tq - 1) // tkc + 1
    carry = sweep(0, n_full, carry, False)
    m_fin, l_fin = sweep(n_full, n_need, carry, True)
    o_ref[...] = (acc_scr[...] / l_fin).T.astype(o_ref.dtype)


def fox_attention(q_src, q_col, k_src, k_col, v_src, v_col, F_all, n_heads, T):
    B = q_src.shape[0]
    Tk = k_src.shape[1]
    past = Tk - T
    tq = min(T, 512)
    tkc = 256 if Tk % 256 == 0 else Tk
    assert T % tq == 0
    f_heads = F_all.transpose(0, 2, 1)
    f_row = f_heads[:, :, None, past:]
    f_rep = jnp.broadcast_to(f_heads[..., None], (B, n_heads, Tk, LANES))

    def kv(col0):
        return pl.BlockSpec((None, Tk, DH_D), lambda b, h, i: (b, 0, col0 + h))

    return pl.pallas_call(
        functools.partial(_fox_kernel, tq=tq, tkc=tkc, past=past),
        grid=(B, n_heads, T // tq),
        in_specs=[pl.BlockSpec((None, tq, DH_D), lambda b, h, i: (b, i, q_col + h)),
                  kv(k_col), kv(v_col),
                  pl.BlockSpec((None, None, 1, tq), lambda b, h, i: (b, h, 0, i)),
                  pl.BlockSpec((None, None, Tk, LANES), lambda b, h, i: (b, h, 0, 0))],
        out_specs=pl.BlockSpec((None, tq, DH_D), lambda b, h, i: (b, i, h)),
        out_shape=jax.ShapeDtypeStruct((B, T, n_heads * DH_D), BF16),
        scratch_shapes=[pltpu.VMEM((Tk, DH_D), BF16), pltpu.VMEM((DH_D, Tk), BF16), pltpu.VMEM((DH_D, tq), F32)],
        compiler_params=_params(("parallel", "parallel", "arbitrary"), 40),
        name="fox_attention",
    )(q_src, k_src, v_src, f_row, f_rep)


SUBLANES = 8


def _bitonic_merge_desc(v):
    v = list(v)
    n = len(v)
    step = n // 2
    while step >= 1:
        for i in range(n):
            if i & step == 0:
                a, b = v[i], v[i + step]
                v[i], v[i + step] = jnp.maximum(a, b), jnp.minimum(a, b)
        step //= 2
    return v


def _bitonic_sort_desc(v):
    v = list(v)
    n = len(v)
    size = 2
    while size <= n:
        step = size // 2
        while step >= 1:
            for i in range(n):
                if i & step == 0:
                    a, b = v[i], v[i + step]
                    hi, lo = jnp.maximum(a, b), jnp.minimum(a, b)
                    v[i], v[i + step] = (hi, lo) if i & size == 0 else (lo, hi)
            step //= 2
        size *= 2
    return v


def _merge_top(a, b):
    n = len(a)
    c = [jnp.maximum(a[i], b[n - 1 - i]) if n - 1 - i < len(b) else a[i] for i in range(n)]
    return _bitonic_merge_desc(c)


def _top_of_keys(s):
    slots = [s[SUBLANES * j:SUBLANES * (j + 1), :] for j in range(s.shape[0] // SUBLANES)]
    assert len(slots) == TOPK_P
    v = _bitonic_sort_desc(slots)
    shift = SUBLANES // 2
    while shift >= 1:
        partner = [pltpu.roll(x, shift, axis=0) for x in v]
        v = _merge_top(v, partner)
        shift //= 2
    return v


def _peer_score_kernel(h_ref, wq_ref, sk_ref, s1_ref, s2_ref, tau_ref, mz_ref, *, n_heads, n_keys):
    qT = lax.dot_general(wq_ref[...], h_ref[...], (((1,), (1,)), ((), ())), preferred_element_type=F32)
    dq = qT.shape[0] // (2 * n_heads)
    t = qT.shape[1]
    head_row = lax.broadcasted_iota(jnp.int32, (SUBLANES, t), 0)
    tops = [[jnp.zeros((SUBLANES, t), F32)] * TOPK_P for _ in range(2)]
    for hd in range(n_heads):
        for c in range(2):
            hc = 2 * hd + c
            q = qT[hc * dq:(hc + 1) * dq, :].astype(BF16)
            s = jnp.dot(sk_ref[hc], q, preferred_element_type=F32)
            if c == 0:
                s1_ref[:, hd, :] = s
            else:
                s2_ref[hd] = s
            top = _top_of_keys(s)
            tops[c] = [jnp.where(head_row == hd, top[i], tops[c][i]) for i in range(TOPK_P)]
    v1, v2 = tops
    best = [v1[0] + v2[b] for b in range(TOPK_P)]
    for a in range(1, TOPK_P):
        best = _merge_top(best, [v1[a] + v2[b] for b in range(TOPK_P // (a + 1))])
    z = jnp.zeros_like(best[0])
    for s_k in best:
        z = z + jnp.exp(s_k - best[0])
    tau_ref[...] = best[TOPK_P - 1]
    mz_ref[...] = best[0] + jnp.log(z)


def peer_scores(h2d, wq_t, subkeys_b):
    N, d = h2d.shape
    n_hc, n_keys, dq = subkeys_b.shape
    n_heads = n_hc // 2
    assert n_heads == SUBLANES and n_keys == TOPK_P * SUBLANES
    tt = min(N, 256)
    return pl.pallas_call(
        functools.partial(_peer_score_kernel, n_heads=n_heads, n_keys=n_keys),
        grid=(N // tt,),
        in_specs=[pl.BlockSpec((tt, d), lambda i: (i, 0)),
                  pl.BlockSpec(wq_t.shape, lambda i: (0, 0), pipeline_mode=pl.Buffered(1)),
                  pl.BlockSpec(subkeys_b.shape, lambda i: (0, 0, 0))],
        out_specs=[pl.BlockSpec((n_keys, n_heads, tt), lambda i: (0, 0, i)),
                   pl.BlockSpec((n_heads, n_keys, tt), lambda i: (0, 0, i)),
                   pl.BlockSpec((n_heads, tt), lambda i: (0, i)),
                   pl.BlockSpec((n_heads, tt), lambda i: (0, i))],
        out_shape=[jax.ShapeDtypeStruct((n_keys, n_heads, N), F32),
                   jax.ShapeDtypeStruct((n_heads, n_keys, N), F32),
                   jax.ShapeDtypeStruct((n_heads, N), F32),
                   jax.ShapeDtypeStruct((n_heads, N), F32)],
        compiler_params=_params(("parallel",), 48),
        name="peer_scores",
    )(h2d, wq_t, subkeys_b)


def _peer_expert_kernel(h_ref, u_ref, vt_ref, s1_ref, s2_ref, tau_ref, mz_ref, o_ref, w0_scr, w1_scr, g0_scr, g1_scr,
                        *, n_heads, n_keys, rows, n_tiles):
    j = pl.program_id(1)
    tt = h_ref.shape[0]
    d = o_ref.shape[0]
    cw = min(tt, LANES)
    w_scr = (w0_scr, w1_scr)
    g_scr = (g0_scr, g1_scr)
    gate_pieces = [(r, c0) for r in range(rows) for c0 in range(0, tt, cw)]
    n_g = len(gate_pieces)

    def gate_piece(tile, g_ref, r, c0):
        cols = slice(c0, c0 + cw)
        s1_heads = s1_ref[tile * rows + r]
        g = jnp.zeros((n_keys, cw), F32)
        for hd in range(n_heads):
            s = s1_heads[hd:hd + 1, cols] + s2_ref[hd, :, cols]
            e = jnp.exp(s - mz_ref[hd:hd + 1, cols])
            g = g + jnp.where(s >= tau_ref[hd:hd + 1, cols], e, 0.0)
        g_ref[r * n_keys:(r + 1) * n_keys, cols] = g

    def score_piece(g_ref, w_ref, piece, n_p):
        cols = slice(piece * (tt // n_p), (piece + 1) * (tt // n_p))
        hT = lax.dot_general(u_ref[...], h_ref[cols, :], (((1,), (1,)), ((), ())), preferred_element_type=F32)
        w_ref[:, cols] = (jax.nn.gelu(hT) * g_ref[:, cols]).astype(BF16)

    def accumulate_piece(w_ref, m, n_m):
        rws = slice(m * (d // n_m), (m + 1) * (d // n_m))
        o_ref[rws, :] += jnp.dot(vt_ref[rws, :], w_ref[...], preferred_element_type=F32)

    n_sp = 2 if tt >= 2 * LANES else 1

    @pl.when(j == 0)
    def _():
        o_ref[...] = jnp.zeros_like(o_ref)
        for r, c0 in gate_pieces:
            gate_piece(j, g_scr[0], r, c0)
        for p in range(n_sp):
            score_piece(g_scr[0], w_scr[0], p, n_sp)
        for r, c0 in gate_pieces:
            gate_piece(j + 1, g_scr[1], r, c0)

    for par in range(2):
        @pl.when(jnp.logical_and(jnp.logical_and(j > 0, j < n_tiles), j % 2 == par))
        def _():
            nxt = jnp.minimum(j + 1, n_tiles - 1)
            for m in range(n_g):
                accumulate_piece(w_scr[1 - par], m, n_g)
                gate_piece(nxt, g_scr[1 - par], *gate_pieces[m])
            for p in range(n_sp):
                score_piece(g_scr[par], w_scr[par], p, n_sp)

    @pl.when(j == n_tiles)
    def _():
        accumulate_piece(w_scr[(n_tiles - 1) % 2], 0, 1)


def peer_experts(h2d, u_all, vt_all, layer, s1, s2, tau, mz):
    N, d = h2d.shape
    n_exp = u_all.shape[1]
    n_heads, n_keys, _ = s2.shape
    tt = min(N, 512)
    te = 512
    rows = te // n_keys
    n_tiles = n_exp // te
    assert n_tiles % 2 == 0
    once = pl.Buffered(1)
    return pl.pallas_call(
        functools.partial(_peer_expert_kernel, n_heads=n_heads, n_keys=n_keys, rows=rows, n_tiles=n_tiles),
        grid=(N // tt, n_tiles + 1),
        in_specs=[pl.BlockSpec((tt, d), lambda i, j: (i, 0), pipeline_mode=once),
                  pl.BlockSpec((None, te, d), lambda i, j: (layer, jnp.minimum(j, n_tiles - 1), 0)),
                  pl.BlockSpec((None, d, te), lambda i, j: (layer, 0, jnp.maximum(j - 1, 0))),
                  pl.BlockSpec((n_keys, n_heads, tt), lambda i, j: (0, 0, i), pipeline_mode=once),
                  pl.BlockSpec((n_heads, n_keys, tt), lambda i, j: (0, 0, i), pipeline_mode=once),
                  pl.BlockSpec((n_heads, tt), lambda i, j: (0, i), pipeline_mode=once),
                  pl.BlockSpec((n_heads, tt), lambda i, j: (0, i), pipeline_mode=once)],
        out_specs=pl.BlockSpec((d, tt), lambda i, j: (0, i)),
        out_shape=jax.ShapeDtypeStruct((d, N), F32),
        scratch_shapes=[pltpu.VMEM((te, tt), BF16)] * 2 + [pltpu.VMEM((te, tt), F32)] * 2,
        compiler_params=_params(("parallel", "arbitrary"), 56),
        name="peer_experts",
    )(h2d, u_all, vt_all, s1, s2, tau, mz)


def peer_ffn(h, wq_t, subkeys_b, u_all, vt_all, layer):
    B, T, d = h.shape
    h2d = h.reshape(B * T, d)
    s1, s2, tau, mz = peer_scores(h2d, wq_t, subkeys_b)
    return peer_experts(h2d, u_all, vt_all, layer, s1, s2, tau, mz)


def _pad_cols(w, n):
    return jnp.pad(w, ((0, 0), (0, n - w.shape[1])))


def kernel(x_prompt, x_sample, c_prompt, c_sample, state_rwkv_shift, state_rwkv, state_ret, cache_fox_k, cache_fox_v, cache_fox_logf, ada_w, ada_b, norm_g, final_g, w_in_even, w_out_even, sgu_norm_g, sgu_w, sgu_b, rwkv_mu, rwkv_w0, rwkv_w_up, rwkv_a0, rwkv_a_up, rwkv_g_up, rwkv_k_k, rwkv_k_a, rwkv_r_k, rwkv_ln_w, rwkv_ln_b, w_in_odd, w_out_odd, ret_gn_g, fox_b_f, peer_wq, peer_subkeys, peer_u, peer_v):
    depth = ada_w.shape[0]
    d = x_prompt.shape[-1]
    d_a = d_b = d // 2
    h_b = d_b // DH_B
    h_c = d // (2 * DV_C)
    h_d = d // (2 * DH_D)
    p_b = rwkv_mu.shape[1]
    n_bp, n_bs = x_prompt.shape[0], x_sample.shape[0]
    past_len = cache_fox_k.shape[2]

    n_rows = _round_up(n_bp + n_bs, 16)
    c_all = jnp.pad(jnp.concatenate([c_prompt, c_sample], axis=0), ((0, n_rows - n_bp - n_bs), (0, 0)))
    mod4 = ada_modulation(c_all, ada_w, ada_b).reshape(depth, n_rows, 1, 6 * d)

    w_in_e = [_pad_cols(w_in_even[j], 2 * d_a + 3 * d_b + TAIL_BLOCK).astype(BF16) for j in range(w_in_even.shape[0])]
    w_in_o = [_pad_cols(w_in_odd[j], _round_up(w_in_odd.shape[2], 512)).astype(BF16) for j in range(w_in_odd.shape[0])]
    w_out_e = [w_out_even[j].astype(BF16) for j in range(w_out_even.shape[0])]
    w_out_o = [w_out_odd[j].astype(BF16) for j in range(w_out_odd.shape[0])]
    n_hp = peer_subkeys.shape[1]
    wq_t = [peer_wq[i].T.astype(BF16) for i in range(depth)]
    subk = [peer_subkeys[i].reshape(2 * n_hp, peer_subkeys.shape[3], peer_subkeys.shape[4]).astype(BF16)
            for i in range(depth)]
    u_all = peer_u.astype(BF16)
    vt_all = jnp.swapaxes(peer_v, 1, 2).astype(BF16)

    def run(x, row0, pos0, shift0, rwkv0, ret0, fox_past):
        B, T, _ = x.shape
        pos = pos0 + jnp.arange(T)
        v_rows, shifts, rwkvs, rets, ks, vs, lfs = [], [], [], [], [], [], []
        pending = None
        for i in range(depth):
            j = i // 2
            res = None if pending is None else (pending[0], pending[1], mod4, pending[3], row0, pending[2])
            x_new, h = resnorm(x, norm_g[i, 0], res=res, mod=(mod4, i, row0, 1, 0))
            x = x if x_new is None else x_new
            h2d = h.reshape(B * T, d)
            if i % 2 == 0:
                p = matmul(h2d, w_in_e[j]).reshape(B, T, -1)
                out_a, vn = spatial_gating(p, d_a, sgu_norm_g[j], sgu_w[j], sgu_b[j])
                r, w, k, v, a, g = rwkv_prep(p, d_b, 2 * d_a, shift0[j], rwkv_mu[j], rwkv_w0[j], rwkv_w_up[j],
                                             rwkv_a0[j], rwkv_a_up[j], rwkv_g_up[j])
                yb, S = rwkv_scan(r, w, k, v, a, rwkv0[j], rwkv_k_k[j], rwkv_k_a[j], rwkv_r_k[j], rwkv_ln_w[j],
                                  rwkv_ln_b[j])
                mix = gate_concat(out_a, yb, g)
                y = matmul(mix.reshape(B * T, d), w_out_e[j]).reshape(B, T, d)
                v_rows.append(vn)
                shifts.append(p[:, -1:, 2 * d_a:2 * d_a + p_b])
                rwkvs.append(S)
            else:
                p = matmul(h2d, w_in_o[j]).reshape(B, T, -1)
                c_dq = 2 * h_c * DK_C + 2 * h_c * DV_C
                c_dk, c_dv, c_df = c_dq + h_d * DH_D, c_dq + 2 * h_d * DH_D, c_dq + 3 * h_d * DH_D
                out_c, R = retention(p, h_c, pos, ret0[j], ret_gn_g[j])
                logf = jax.nn.log_sigmoid(p[..., c_df:c_df + h_d] + fox_b_f[j])
                kd = p[..., c_dk:c_dk + h_d * DH_D]
                vd = p[..., c_dv:c_dv + h_d * DH_D]
                if fox_past is None:
                    F_all = jnp.cumsum(logf, axis=1)
                    od = fox_attention(p, c_dq // DH_D, p, c_dk // DH_D, p, c_dv // DH_D, F_all, h_d, T)
                else:
                    k_past, v_past, lf_past = fox_past[j]
                    k_all = jnp.concatenate([k_past.reshape(B, past_len, -1), kd], axis=1)
                    v_all = jnp.concatenate([v_past.reshape(B, past_len, -1), vd], axis=1)
                    F_all = jnp.cumsum(jnp.concatenate([lf_past, logf], axis=1), axis=1)
                    od = fox_attention(p, c_dq // DH_D, k_all, 0, v_all, 0, F_all, h_d, T)
                mix = jnp.concatenate([out_c, od], axis=-1)
                y = matmul(mix.reshape(B * T, d), w_out_o[j]).reshape(B, T, d)
                rets.append(R)
                ks.append(kd.reshape(B, T, h_d, DH_D))
                vs.append(vd.reshape(B, T, h_d, DH_D))
                lfs.append(logf)
            x, h = resnorm(x, norm_g[i, 1], res=(y, False, mod4, i, row0, 2), mod=(mod4, i, row0, 4, 3))
            y_t = peer_ffn(h, wq_t[i], subk[i], u_all, vt_all, i)
            if T % LANES == 0:
                pending = (y_t, True, 5, i)
            else:
                pending = (y_t.T.reshape(B, T, d), False, 5, i)
        _, out = resnorm(x, final_g, res=(pending[0], pending[1], mod4, pending[3], row0, pending[2]),
                         out_dtype=F32)
        return out, v_rows, shifts, rwkvs, rets, ks, vs, lfs

    n_even, n_odd = w_in_even.shape[0], w_in_odd.shape[0]
    y_p, _, sh_p, rw_p, rt_p, k_p, v_p, lf_p = run(
        x_prompt, 0, 0,
        [jnp.zeros((n_bp, 1, p_b), F32)] * n_even,
        [jnp.zeros((n_bp, h_b, DH_B, DH_B), F32)] * n_even,
        [jnp.zeros((n_bp, h_c, DK_C, DV_C), F32)] * n_odd,
        None)
    y_s, vr_s, sh_s, rw_s, rt_s, k_s, v_s, lf_s = run(
        x_sample, n_bp, past_len,
        [state_rwkv_shift[j] for j in range(n_even)],
        [state_rwkv[j] for j in range(n_even)],
        [state_ret[j] for j in range(n_odd)],
        [(cache_fox_k[j], cache_fox_v[j], cache_fox_logf[j]) for j in range(n_odd)])
    return (y_p, y_s, jnp.stack(vr_s), jnp.stack(sh_p), jnp.stack(sh_s), jnp.stack(rw_p), jnp.stack(rw_s),
            jnp.stack(rt_p), jnp.stack(rt_s), jnp.stack(k_p), jnp.stack(k_s), jnp.stack(v_p), jnp.stack(v_s),
            jnp.stack(lf_p), jnp.stack(lf_s))
```

```python
import functools
import math

import jax
import jax.numpy as jnp
import numpy as np
from jax import lax
from jax.experimental import pallas as pl
from jax.experimental.pallas import tpu as pltpu

F32 = jnp.float32
BF16 = jnp.bfloat16

EPS = 1e-6
RWKV_GN_EPS = 64e-5
ROPE_BASE = 10000.0
RET_CHUNK = 64
MLP_CHUNK = 128
N_GROUPS_A = 8
DH_B = 64
DK_C = 128
DV_C = 256
DH_D = 128
TOPK_P = 16

LANES = 128
TAIL_BLOCK = 512
MIB = 1024 * 1024


def _round_up(n, m):
    return (n + m - 1) // m * m


def _params(sem, vmem_mib):
    return pltpu.CompilerParams(dimension_semantics=sem, vmem_limit_bytes=int(vmem_mib * MIB))


def _mm_kernel(a_ref, b_ref, o_ref):
    o_ref[...] = jnp.dot(a_ref[...], b_ref[...], preferred_element_type=F32).astype(o_ref.dtype)


def matmul(a, b, out_dtype=F32):
    M, K = a.shape
    N = b.shape[1]
    tm = min(M, 1024)
    tn = 512
    assert M % tm == 0 and N % tn == 0, (M, N)
    return pl.pallas_call(
        _mm_kernel,
        grid=(M // tm, N // tn),
        in_specs=[pl.BlockSpec((tm, K), lambda i, j: (i, 0)),
                  pl.BlockSpec((K, tn), lambda i, j: (0, j))],
        out_specs=pl.BlockSpec((tm, tn), lambda i, j: (i, j)),
        out_shape=jax.ShapeDtypeStruct((M, N), out_dtype),
        compiler_params=_params(("parallel", "parallel"), 48),
        name="matmul",
    )(a, b)


def _ada_kernel(c_ref, w_ref, b_ref, o_ref):
    c = c_ref[...]
    a = (c * jax.nn.sigmoid(c)).astype(BF16)
    o_ref[...] = jnp.dot(a, w_ref[...].astype(BF16), preferred_element_type=F32) + b_ref[...]


def ada_modulation(c_all, ada_w, ada_b):
    depth, d, n6 = ada_w.shape
    R = c_all.shape[0]
    tn = 512
    return pl.pallas_call(
        _ada_kernel,
        grid=(depth, n6 // tn),
        in_specs=[pl.BlockSpec((R, d), lambda i, j: (0, 0)),
                  pl.BlockSpec((None, d, tn), lambda i, j: (i, 0, j)),
                  pl.BlockSpec((None, 1, tn), lambda i, j: (i, 0, j))],
        out_specs=pl.BlockSpec((None, R, tn), lambda i, j: (i, 0, j)),
        out_shape=jax.ShapeDtypeStruct((depth, R, n6), F32),
        compiler_params=_params(("parallel", "parallel"), 40),
        name="ada_modulation",
    )(c_all, ada_w, ada_b.reshape(depth, 1, n6))


def _resnorm_kernel(*refs, has_res, y_transposed, has_mod):
    it = iter(refs)
    x_ref = next(it)
    if has_res:
        y_ref = next(it)
        gate_ref = next(it)
    g_ref = next(it)
    if has_mod:
        sc_ref = next(it)
        sh_ref = next(it)
    if has_res:
        xo_ref = next(it)
    h_ref = next(it)
    x = x_ref[...]
    if has_res:
        y = y_ref[...]
        if y_transposed:
            y = y.T
        x = x + gate_ref[...] * y
        xo_ref[...] = x
    xn = x * lax.rsqrt(jnp.mean(x * x, axis=-1, keepdims=True) + EPS) * g_ref[...]
    if has_mod:
        xn = xn * (1.0 + sc_ref[...]) + sh_ref[...]
    h_ref[...] = xn.astype(h_ref.dtype)


def resnorm(x, norm_g, *, res=None, mod=None, out_dtype=BF16):
    B, T, d = x.shape
    tt = min(T, 256)
    nt = T // tt
    row = pl.BlockSpec((None, tt, d), lambda b, i: (b, i, 0))
    args, specs = [x], [row]
    if res is not None:
        y, y_transposed, mod4, layer, row0, gk = res
        args.append(y)
        if y_transposed:
            specs.append(pl.BlockSpec((d, tt), lambda b, i: (0, b * nt + i)))
        else:
            specs.append(row)
        args.append(mod4)
        specs.append(pl.BlockSpec((None, None, 1, d), lambda b, i: (layer, row0 + b, 0, gk)))
    else:
        y_transposed = False
    args.append(norm_g.reshape(1, d))
    specs.append(pl.BlockSpec((1, d), lambda b, i: (0, 0)))
    if mod is not None:
        mod4m, layer_m, row0_m, sck, shk = mod
        args += [mod4m, mod4m]
        specs.append(pl.BlockSpec((None, None, 1, d), lambda b, i: (layer_m, row0_m + b, 0, sck)))
        specs.append(pl.BlockSpec((None, None, 1, d), lambda b, i: (layer_m, row0_m + b, 0, shk)))
    out_shape, out_specs = [], []
    if res is not None:
        out_shape.append(jax.ShapeDtypeStruct((B, T, d), F32))
        out_specs.append(row)
    out_shape.append(jax.ShapeDtypeStruct((B, T, d), out_dtype))
    out_specs.append(row)
    outs = pl.pallas_call(
        functools.partial(_resnorm_kernel, has_res=res is not None, y_transposed=y_transposed,
                          has_mod=mod is not None),
        grid=(B, nt),
        in_specs=specs,
        out_specs=out_specs,
        out_shape=out_shape,
        compiler_params=_params(("parallel", "parallel"), 48),
        name="resnorm",
    )(*args)
    if res is not None:
        return outs[0], outs[1]
    return None, outs[0]


def _sgu_kernel(pu_ref, pv_ref, ng_ref, w_ref, b_ref, oa_ref, vn_ref, *, groups):
    u = jax.nn.gelu(pu_ref[...])
    v = jax.nn.gelu(pv_ref[...])
    dg = v.shape[1] // groups
    for g in range(groups):
        sl = slice(g * dg, (g + 1) * dg)
        vg = v[:, sl]
        mean = jnp.mean(vg, axis=-1, keepdims=True)
        cen = vg - mean
        var = jnp.mean(cen * cen, axis=-1, keepdims=True)
        vn = cen * lax.rsqrt(var + EPS) * ng_ref[:, sl]
        vn_ref[:, sl] = vn
        s = jnp.dot(w_ref[g], vn.astype(BF16), preferred_element_type=F32) + b_ref[:, sl]
        oa_ref[:, sl] = (u[:, sl] * s).astype(oa_ref.dtype)


def spatial_gating(p, d_a, norm_g, w_s, b_s):
    B, T, _ = p.shape
    L = min(T, MLP_CHUNK)
    G = w_s.shape[0]
    dg = d_a // G
    w_tril = jnp.tril(w_s)[:, :L, :L].astype(BF16)
    b_full = jnp.repeat(b_s.T[:L], dg, axis=1)
    blk = lambda c: pl.BlockSpec((None, L, d_a), lambda b, i: (b, i, c))
    return pl.pallas_call(
        functools.partial(_sgu_kernel, groups=G),
        grid=(B, T // L),
        in_specs=[blk(0), blk(1),
                  pl.BlockSpec((1, d_a), lambda b, i: (0, 0)),
                  pl.BlockSpec((G, L, L), lambda b, i: (0, 0, 0)),
                  pl.BlockSpec((L, d_a), lambda b, i: (0, 0))],
        out_specs=[blk(0), blk(0)],
        out_shape=[jax.ShapeDtypeStruct((B, T, d_a), BF16), jax.ShapeDtypeStruct((B, T, d_a), F32)],
        compiler_params=_params(("parallel", "parallel"), 32),
        name="spatial_gating",
    )(p, p, norm_g.reshape(1, d_a), w_tril, b_full)


def _shifted(x_ref, halo_ref, state_ref, mu_ref):
    x = x_ref[...]
    first = jnp.where(pl.program_id(1) == 0, state_ref[...], halo_ref[7:8, :])
    rows = lax.broadcasted_iota(jnp.int32, x.shape, 0)
    prev = jnp.where(rows == 0, first, pltpu.roll(x, 1, axis=0))
    return x + (prev - x) * mu_ref[...]


def _rwkv_prep_kernel(r_ref, k_ref, v_ref, t_ref, rh_ref, kh_ref, vh_ref, th_ref,
                      sr_ref, sk_ref, sv_ref, st_ref, mr_ref, mk_ref, mv_ref, mt_ref,
                      w0_ref, wup_ref, a0_ref, aup_ref, gup_ref,
                      ro_ref, wo_ref, ko_ref, vo_ref, ao_ref, go_ref, *, lw, la, lg):
    r = _shifted(r_ref, rh_ref, sr_ref, mr_ref)
    k = _shifted(k_ref, kh_ref, sk_ref, mk_ref)
    v = _shifted(v_ref, vh_ref, sv_ref, mv_ref)
    t = _shifted(t_ref, th_ref, st_ref, mt_ref)
    wd, ad, gd = t[:, :lw], t[:, lw:lw + la], t[:, lw + la:lw + la + lg]
    wl = w0_ref[...] + jnp.dot(jnp.tanh(wd).astype(BF16), wup_ref[...], preferred_element_type=F32)
    w_log = -jax.nn.softplus(-wl) - 0.5
    wo_ref[...] = jnp.exp(-jnp.exp(w_log))
    a = jax.nn.sigmoid(a0_ref[...] + jnp.dot(ad.astype(BF16), aup_ref[...], preferred_element_type=F32))
    go_ref[...] = jnp.dot(jax.nn.sigmoid(gd).astype(BF16), gup_ref[...], preferred_element_type=F32)
    ro_ref[...] = r
    vo_ref[...] = v
    ao_ref[...] = a
    ko_ref[...] = k


def rwkv_prep(p, d_b, col0, shift_state, mu, w0, w_up, a0, a_up, g_up):
    B, T, _ = p.shape
    lw, la, lg = w_up.shape[0], a_up.shape[0], g_up.shape[0]
    tt = min(T, 128)
    cb = col0 // d_b
    tb = (col0 + 3 * d_b) // TAIL_BLOCK
    assert col0 % d_b == 0 and (col0 + 3 * d_b) % TAIL_BLOCK == 0 and lw + la + lg <= TAIL_BLOCK
    pad = TAIL_BLOCK - (lw + la + lg)

    def seg(c, width):
        return pl.BlockSpec((None, tt, width), lambda b, i: (b, i, c))

    def halo(c, width):
        return pl.BlockSpec((None, 8, width), lambda b, i: (b, jnp.maximum(i * (tt // 8) - 1, 0), c))

    def per_b(width):
        return pl.BlockSpec((None, 1, width), lambda b, i: (b, 0, 0))

    def full(shape):
        return pl.BlockSpec(shape, lambda b, i: tuple(0 for _ in shape))

    st = shift_state
    s_parts = [st[..., :d_b], st[..., d_b:2 * d_b], st[..., 2 * d_b:3 * d_b],
               jnp.pad(st[..., 3 * d_b:], ((0, 0), (0, 0), (0, pad)))]
    m_parts = [mu[None, :d_b], mu[None, d_b:2 * d_b], mu[None, 2 * d_b:3 * d_b],
               jnp.pad(mu[None, 3 * d_b:], ((0, 0), (0, pad)))]
    out_row = pl.BlockSpec((None, tt, d_b), lambda b, i: (b, i, 0))
    outs = pl.pallas_call(
        functools.partial(_rwkv_prep_kernel, lw=lw, la=la, lg=lg),
        grid=(B, T // tt),
        in_specs=[seg(cb, d_b), seg(cb + 1, d_b), seg(cb + 2, d_b), seg(tb, TAIL_BLOCK),
                  halo(cb, d_b), halo(cb + 1, d_b), halo(cb + 2, d_b), halo(tb, TAIL_BLOCK),
                  per_b(d_b), per_b(d_b), per_b(d_b), per_b(TAIL_BLOCK),
                  full((1, d_b)), full((1, d_b)), full((1, d_b)), full((1, TAIL_BLOCK)),
                  full((1, d_b)), full((lw, d_b)), full((1, d_b)), full((la, d_b)), full((lg, d_b))],
        out_specs=[out_row] * 6,
        out_shape=[jax.ShapeDtypeStruct((B, T, d_b), F32)] * 6,
        compiler_params=_params(("parallel", "arbitrary"), 56),
        name="rwkv_prep",
    )(p, p, p, p, p, p, p, p, *s_parts, *m_parts,
      w0[None], w_up.astype(BF16), a0[None], a_up.astype(BF16), g_up.astype(BF16))
    return outs


def _rwkv_scan_kernel(r_ref, w_ref, k_ref, v_ref, a_ref, kk_ref, ka_ref, rk_ref, lnw_ref, lnb_ref, s0_ref,
                      y_ref, so_ref, s_scr, av_scr, bv_scr, k2_scr, *, tc, dh):
    @pl.when(pl.program_id(1) == 0)
    def _():
        s_scr[...] = s0_ref[...]

    def step(t, carry):
        a = a_ref[t]
        kr = k_ref[t] * kk_ref[...]
        kk = kr * lax.rsqrt(jnp.maximum(jnp.sum(kr * kr, axis=0, keepdims=True), 1e-24))
        av_scr[...] = -kk
        bv_scr[...] = kk * a
        k2 = k_ref[t] * (1.0 + (a - 1.0) * ka_ref[...])
        k2_scr[...] = k2
        v = v_ref[t]
        sa = jnp.zeros_like(v)
        for k in range(dh):
            sa = sa + s_scr[k] * av_scr[k:k + 1, :]
        y = jnp.zeros_like(v)
        for k in range(dh):
            s_new = (s_scr[k] * w_ref[t, k:k + 1, :] + sa * bv_scr[k:k + 1, :] + v * k2_scr[k:k + 1, :])
            s_scr[k] = s_new
            y = y + s_new * r_ref[t, k:k + 1, :]
        mean = jnp.mean(y, axis=0, keepdims=True)
        cen = y - mean
        var = jnp.mean(cen * cen, axis=0, keepdims=True)
        yn = cen * lax.rsqrt(var + RWKV_GN_EPS) * lnw_ref[...] + lnb_ref[...]
        bonus = jnp.sum(r_ref[t] * k2 * rk_ref[...], axis=0, keepdims=True) * v
        y_ref[t] = yn + bonus
        return carry

    lax.fori_loop(0, tc, step, 0)

    @pl.when(pl.program_id(1) == pl.num_programs(1) - 1)
    def _():
        so_ref[...] = s_scr[...]


def rwkv_scan(r, w, k, v, a, S0, k_k, k_a, r_k, ln_w, ln_b):
    B, T, d_b = r.shape
    H = d_b // DH_B
    BH = B * H
    BHp = _round_up(BH, LANES)

    def to_lanes(x):
        x = x.reshape(B, T, H, DH_B).transpose(1, 3, 0, 2).reshape(T, DH_B, BH)
        return jnp.pad(x, ((0, 0), (0, 0), (0, BHp - BH)))

    def param_lanes(x):
        x = jnp.tile(x.T, (1, B))
        return jnp.pad(x, ((0, 0), (0, BHp - BH)))

    s0 = S0.transpose(3, 2, 0, 1).reshape(DH_B, DH_B, BH)
    s0 = jnp.pad(s0, ((0, 0), (0, 0), (0, BHp - BH)))
    tc = min(T, 32)
    seq = pl.BlockSpec((tc, DH_B, LANES), lambda l, i: (i, 0, l))
    par = pl.BlockSpec((DH_B, LANES), lambda l, i: (0, l))
    st = pl.BlockSpec((DH_B, DH_B, LANES), lambda l, i: (0, 0, l))
    y, s_fin = pl.pallas_call(
        functools.partial(_rwkv_scan_kernel, tc=tc, dh=DH_B),
        grid=(BHp // LANES, T // tc),
        in_specs=[seq] * 5 + [par] * 5 + [st],
        out_specs=[seq, st],
        out_shape=[jax.ShapeDtypeStruct((T, DH_B, BHp), F32), jax.ShapeDtypeStruct((DH_B, DH_B, BHp), F32)],
        scratch_shapes=[pltpu.VMEM((DH_B, DH_B, LANES), F32)] + [pltpu.VMEM((DH_B, LANES), F32)] * 3,
        compiler_params=_params(("parallel", "arbitrary"), 40),
        name="rwkv_scan",
    )(to_lanes(r), to_lanes(w), to_lanes(k), to_lanes(v), to_lanes(a),
      param_lanes(k_k.reshape(H, DH_B)), param_lanes(k_a.reshape(H, DH_B)), param_lanes(r_k),
      param_lanes(ln_w.reshape(H, DH_B)), param_lanes(ln_b.reshape(H, DH_B)), s0)
    y = y[:, :, :BH].reshape(T, DH_B, B, H).transpose(2, 0, 3, 1).reshape(B, T, d_b)
    s_fin = s_fin[:, :, :BH].reshape(DH_B, DH_B, B, H).transpose(2, 3, 1, 0)
    return y, s_fin


def _gate_concat_kernel(oa_ref, yb_ref, g_ref, o_ref, *, d_a):
    o_ref[:, :d_a] = oa_ref[...]
    o_ref[:, d_a:] = (yb_ref[...] * g_ref[...]).astype(o_ref.dtype)


def gate_concat(out_a, yb, g):
    B, T, d_a = out_a.shape
    d_b = yb.shape[-1]
    tt = min(T, 256)
    blk = lambda w: pl.BlockSpec((None, tt, w), lambda b, i: (b, i, 0))
    return pl.pallas_call(
        functools.partial(_gate_concat_kernel, d_a=d_a),
        grid=(B, T // tt),
        in_specs=[blk(d_a), blk(d_b), blk(d_b)],
        out_specs=blk(d_a + d_b),
        out_shape=jax.ShapeDtypeStruct((B, T, d_a + d_b), BF16),
        compiler_params=_params(("parallel", "parallel"), 32),
        name="gate_concat",
    )(out_a, yb, g)


def _rope(x, cos2, sin2):
    return x * cos2 + pltpu.roll(x, x.shape[-1] // 2, axis=1) * sin2


def _retention_kernel(q_ref, k_ref, v_ref, g_ref, cos_ref, sin_ref, dm_ref, kd_ref, qd_ref, gl_ref, gn_ref,
                      r0_ref, o_ref, ro_ref, r_scr, *, L, n_chunks):
    @pl.when(pl.program_id(2) == 0)
    def _():
        r_scr[...] = r0_ref[...]

    scale = DK_C ** -0.5
    for c in range(n_chunks):
        rows = slice(c * L, (c + 1) * L)
        cos2, sin2 = cos_ref[rows, :], sin_ref[rows, :]
        q = _rope(q_ref[rows, :], cos2, sin2)
        k = _rope(k_ref[rows, :], cos2, sin2) * scale
        v = v_ref[rows, :].astype(BF16)
        att = lax.dot_general(q.astype(BF16), k.astype(BF16), (((1,), (1,)), ((), ())),
                              preferred_element_type=F32) * dm_ref[...]
        R = r_scr[...]
        o = jnp.dot(att.astype(BF16), v, preferred_element_type=F32)
        o = o + jnp.dot((q * qd_ref[...]).astype(BF16), R.astype(BF16), preferred_element_type=F32)
        kd_t = (k * kd_ref[...]).T.astype(BF16)
        r_scr[...] = R * gl_ref[...] + jnp.dot(kd_t, v, preferred_element_type=F32)
        mean = jnp.mean(o, axis=-1, keepdims=True)
        cen = o - mean
        var = jnp.mean(cen * cen, axis=-1, keepdims=True)
        on = cen * lax.rsqrt(var + EPS) * gn_ref[...]
        gate = g_ref[rows, :]
        o_ref[rows, :] = (gate * jax.nn.sigmoid(gate) * on).astype(o_ref.dtype)

    @pl.when(pl.program_id(2) == pl.num_programs(2) - 1)
    def _():
        ro_ref[...] = r_scr[...]


def retention(p, h_c, pos, R0, gn_g):
    B, T, _ = p.shape
    L = math.gcd(T, RET_CHUNK)
    tr = min(T, 512)
    n_chunks = tr // L
    half = DK_C // 2
    inv = ROPE_BASE ** (-jnp.arange(half, dtype=F32) / half)
    ang = pos.astype(F32)[:, None] * inv[None, :]
    cos2 = jnp.concatenate([jnp.cos(ang), jnp.cos(ang)], axis=-1)
    sin2 = jnp.concatenate([-jnp.sin(ang), jnp.sin(ang)], axis=-1)
    log_g = jnp.log1p(-jnp.exp2(-5.0 - jnp.arange(h_c, dtype=F32)))
    idx = jnp.arange(L, dtype=F32)
    rel = idx[:, None] - idx[None, :]
    dmat = jnp.where(rel >= 0, jnp.exp(jnp.maximum(rel, 0.0) * log_g[:, None, None]), 0.0)
    k_dec = jnp.exp((L - 1 - idx)[None, :, None] * log_g[:, None, None])
    q_dec = jnp.exp((idx + 1)[None, :, None] * log_g[:, None, None])
    k_dec = jnp.broadcast_to(k_dec, (h_c, L, DK_C))
    q_dec = jnp.broadcast_to(q_dec, (h_c, L, DK_C))
    g_l = jnp.broadcast_to(jnp.exp(L * log_g)[:, None, None], (h_c, 1, DV_C))

    def col(width, c0):
        return pl.BlockSpec((None, tr, width), lambda b, h, i: (b, i, c0 + h))

    def head(shape):
        return pl.BlockSpec((None,) + shape, lambda b, h, i: (h,) + tuple(0 for _ in shape))

    trig = pl.BlockSpec((tr, DK_C), lambda b, h, i: (i, 0))
    state = pl.BlockSpec((None, None, DK_C, DV_C), lambda b, h, i: (b, h, 0, 0))
    return pl.pallas_call(
        functools.partial(_retention_kernel, L=L, n_chunks=n_chunks),
        grid=(B, h_c, T // tr),
        in_specs=[col(DK_C, 0), col(DK_C, h_c), col(DV_C, h_c), col(DV_C, 2 * h_c), trig, trig,
                  head((L, L)), head((L, DK_C)), head((L, DK_C)), head((1, DV_C)),
                  pl.BlockSpec((1, DV_C), lambda b, h, i: (0, h)), state],
        out_specs=[pl.BlockSpec((None, tr, DV_C), lambda b, h, i: (b, i, h)), state],
        out_shape=[jax.ShapeDtypeStruct((B, T, h_c * DV_C), BF16),
                   jax.ShapeDtypeStruct((B, h_c, DK_C, DV_C), F32)],
        scratch_shapes=[pltpu.VMEM((DK_C, DV_C), F32)],
        compiler_params=_params(("parallel", "parallel", "arbitrary"), 32),
        name="retention",
    )(p, p, p, p, cos2, sin2, dmat, k_dec, q_dec, g_l, gn_g.reshape(1, h_c * DV_C), R0)


def _fox_kernel(q_ref, k_ref, v_ref, fq_ref, fk_ref, o_ref, kb_scr, vt_scr, acc_scr, *, tq, tkc, past):
    qi = pl.program_id(2)
    n_kv = k_ref.shape[0] // tkc

    @pl.when(qi == 0)
    def _():
        for c in range(n_kv):
            rows = slice(c * tkc, (c + 1) * tkc)
            kb_scr[rows, :] = k_ref[rows, :].astype(BF16)
            vt_scr[:, rows] = v_ref[rows, :].T.astype(BF16)

    q = q_ref[...].astype(BF16)
    fq = fq_ref[...]
    scale = DH_D ** -0.5
    n_rep = max(tq // LANES, 1)

    def scores(c, masked):
        r0 = 0 if n_kv == 1 else pl.multiple_of(c * tkc, tkc)
        s = lax.dot_general(kb_scr[pl.ds(r0, tkc), :], q, (((1,), (1,)), ((), ())), preferred_element_type=F32)
        fk = fk_ref[pl.ds(r0, tkc), :]
        fk = jnp.concatenate([fk] * n_rep, axis=1) if tq >= LANES else fk[:, :tq]
        s = s * scale + (fq - fk)
        if masked:
            kpos = r0 + lax.broadcasted_iota(jnp.int32, s.shape, 0)
            qpos = past + qi * tq + lax.broadcasted_iota(jnp.int32, s.shape, 1)
            s = jnp.where(kpos <= qpos, s, -jnp.inf)
        return r0, s

    def chunks(cs, carry, masked):
        m_old, l_old = carry
        parts = [scores(c, masked) for c in cs]
        m_new = m_old
        for _, s in parts:
            m_new = jnp.maximum(m_new, jnp.max(s, axis=0, keepdims=True))
        alpha = jnp.exp(m_old - m_new)
        l_new = alpha * l_old
        upd = alpha * acc_scr[...]
        for r0, s in parts:
            p = jnp.exp(s - m_new)
            l_new = l_new + jnp.sum(p, axis=0, keepdims=True)
            upd = upd + jnp.dot(vt_scr[:, pl.ds(r0, tkc)], p.astype(BF16), preferred_element_type=F32)
        acc_scr[...] = upd
        return m_new, l_new

    def sweep(lo, hi, carry, masked):
        for width in (4, 2, 1):
            n_grp = (hi - lo) // width
            carry = lax.fori_loop(
                0, n_grp, lambda i, cr, lo=lo, width=width: chunks(tuple(lo + width * i + u for u in range(width)),
                                                                    cr, masked), carry)
            lo = lo + width * n_grp
        return carry

    acc_scr[...] = jnp.zeros_like(acc_scr)
    carry = (jnp.full((1, tq), -jnp.inf, F32), jnp.zeros((1, tq), F32))
    n_full = (past + qi * tq + 1) // tkc
    n_need = (past + qi * tq + tq - 1) // tkc + 1
    carry = sweep(0, n_full, carry, False)
    m_fin, l_fin = sweep(n_full, n_need, carry, True)
    o_ref[...] = (acc_scr[...] / l_fin).T.astype(o_ref.dtype)


def fox_attention(q_src, q_col, k_src, k_col, v_src, v_col, F_all, n_heads, T):
    B = q_src.shape[0]
    Tk = k_src.shape[1]
    past = Tk - T
    tq = min(T, 512)
    tkc = 256 if Tk % 256 == 0 else Tk
    assert T % tq == 0
    f_heads = F_all.transpose(0, 2, 1)
    f_row = f_heads[:, :, None, past:]
    f_rep = jnp.broadcast_to(f_heads[..., None], (B, n_heads, Tk, LANES))

    def kv(col0):
        return pl.BlockSpec((None, Tk, DH_D), lambda b, h, i: (b, 0, col0 + h))

    return pl.pallas_call(
        functools.partial(_fox_kernel, tq=tq, tkc=tkc, past=past),
        grid=(B, n_heads, T // tq),
        in_specs=[pl.BlockSpec((None, tq, DH_D), lambda b, h, i: (b, i, q_col + h)),
                  kv(k_col), kv(v_col),
                  pl.BlockSpec((None, None, 1, tq), lambda b, h, i: (b, h, 0, i)),
                  pl.BlockSpec((None, None, Tk, LANES), lambda b, h, i: (b, h, 0, 0))],
        out_specs=pl.BlockSpec((None, tq, DH_D), lambda b, h, i: (b, i, h)),
        out_shape=jax.ShapeDtypeStruct((B, T, n_heads * DH_D), BF16),
        scratch_shapes=[pltpu.VMEM((Tk, DH_D), BF16), pltpu.VMEM((DH_D, Tk), BF16), pltpu.VMEM((DH_D, tq), F32)],
        compiler_params=_params(("parallel", "parallel", "arbitrary"), 40),
        name="fox_attention",
    )(q_src, k_src, v_src, f_row, f_rep)


SUBLANES = 8


def _bitonic_merge_desc(v):
    v = list(v)
    n = len(v)
    step = n // 2
    while step >= 1:
        for i in range(n):
            if i & step == 0:
                a, b = v[i], v[i + step]
                v[i], v[i + step] = jnp.maximum(a, b), jnp.minimum(a, b)
        step //= 2
    return v


def _bitonic_sort_desc(v):
    v = list(v)
    n = len(v)
    size = 2
    while size <= n:
        step = size // 2
        while step >= 1:
            for i in range(n):
                if i & step == 0:
                    a, b = v[i], v[i + step]
                    hi, lo = jnp.maximum(a, b), jnp.minimum(a, b)
                    v[i], v[i + step] = (hi, lo) if i & size == 0 else (lo, hi)
            step //= 2
        size *= 2
    return v


def _merge_top(a, b):
    n = len(a)
    c = [jnp.maximum(a[i], b[n - 1 - i]) if n - 1 - i < len(b) else a[i] for i in range(n)]
    return _bitonic_merge_desc(c)


def _top_of_keys(s):
    slots = [s[SUBLANES * j:SUBLANES * (j + 1), :] for j in range(s.shape[0] // SUBLANES)]
    assert len(slots) == TOPK_P
    v = _bitonic_sort_desc(slots)
    shift = SUBLANES // 2
    while shift >= 1:
        partner = [pltpu.roll(x, shift, axis=0) for x in v]
        v = _merge_top(v, partner)
        shift //= 2
    return v


def _peer_score_kernel(h_ref, wq_ref, sk_ref, s1_ref, s2_ref, tau_ref, mz_ref, *, n_heads, n_keys):
    qT = lax.dot_general(wq_ref[...], h_ref[...], (((1,), (1,)), ((), ())), preferred_element_type=F32)
    dq = qT.shape[0] // (2 * n_heads)
    t = qT.shape[1]
    head_row = lax.broadcasted_iota(jnp.int32, (SUBLANES, t), 0)
    tops = [[jnp.zeros((SUBLANES, t), F32)] * TOPK_P for _ in range(2)]
    for hd in range(n_heads):
        for c in range(2):
            hc = 2 * hd + c
            q = qT[hc * dq:(hc + 1) * dq, :].astype(BF16)
            s = jnp.dot(sk_ref[hc], q, preferred_element_type=F32)
            if c == 0:
                s1_ref[:, hd, :] = s
            else:
                s2_ref[hd] = s
            top = _top_of_keys(s)
            tops[c] = [jnp.where(head_row == hd, top[i], tops[c][i]) for i in range(TOPK_P)]
    v1, v2 = tops
    best = [v1[0] + v2[b] for b in range(TOPK_P)]
    for a in range(1, TOPK_P):
        best = _merge_top(best, [v1[a] + v2[b] for b in range(TOPK_P // (a + 1))])
    z = jnp.zeros_like(best[0])
    for s_k in best:
        z = z + jnp.exp(s_k - best[0])
    tau_ref[...] = best[TOPK_P - 1]
    mz_ref[...] = best[0] + jnp.log(z)


def peer_scores(h2d, wq_t, subkeys_b):
    N, d = h2d.shape
    n_hc, n_keys, dq = subkeys_b.shape
    n_heads = n_hc // 2
    assert n_heads == SUBLANES and n_keys == TOPK_P * SUBLANES
    tt = min(N, 256)
    return pl.pallas_call(
        functools.partial(_peer_score_kernel, n_heads=n_heads, n_keys=n_keys),
        grid=(N // tt,),
        in_specs=[pl.BlockSpec((tt, d), lambda i: (i, 0)),
                  pl.BlockSpec(wq_t.shape, lambda i: (0, 0), pipeline_mode=pl.Buffered(1)),
                  pl.BlockSpec(subkeys_b.shape, lambda i: (0, 0, 0))],
        out_specs=[pl.BlockSpec((n_keys, n_heads, tt), lambda i: (0, 0, i)),
                   pl.BlockSpec((n_heads, n_keys, tt), lambda i: (0, 0, i)),
                   pl.BlockSpec((n_heads, tt), lambda i: (0, i)),
                   pl.BlockSpec((n_heads, tt), lambda i: (0, i))],
        out_shape=[jax.ShapeDtypeStruct((n_keys, n_heads, N), F32),
                   jax.ShapeDtypeStruct((n_heads, n_keys, N), F32),
                   jax.ShapeDtypeStruct((n_heads, N), F32),
                   jax.ShapeDtypeStruct((n_heads, N), F32)],
        compiler_params=_params(("parallel",), 48),
        name="peer_scores",
    )(h2d, wq_t, subkeys_b)


def _peer_expert_kernel(h_ref, u_ref, vt_ref, s1_ref, s2_ref, tau_ref, mz_ref, o_ref, w0_scr, w1_scr, g0_scr, g1_scr,
                        *, n_heads, n_keys, rows, n_tiles):
    j = pl.program_id(1)
    tt = h_ref.shape[0]
    d = o_ref.shape[0]
    cw = min(tt, LANES)
    w_scr = (w0_scr, w1_scr)
    g_scr = (g0_scr, g1_scr)
    gate_pieces = [(r, c0) for r in range(rows) for c0 in range(0, tt, cw)]
    n_g = len(gate_pieces)

    def gate_piece(tile, g_ref, r, c0):
        cols = slice(c0, c0 + cw)
        s1_heads = s1_ref[tile * rows + r]
        g = jnp.zeros((n_keys, cw), F32)
        for hd in range(n_heads):
            s = s1_heads[hd:hd + 1, cols] + s2_ref[hd, :, cols]
            e = jnp.exp(s - mz_ref[hd:hd + 1, cols])
            g = g + jnp.where(s >= tau_ref[hd:hd + 1, cols], e, 0.0)
        g_ref[r * n_keys:(r + 1) * n_keys, cols] = g

    def score_piece(g_ref, w_ref, piece, n_p):
        cols = slice(piece * (tt // n_p), (piece + 1) * (tt // n_p))
        hT = lax.dot_general(u_ref[...], h_ref[cols, :], (((1,), (1,)), ((), ())), preferred_element_type=F32)
        w_ref[:, cols] = (jax.nn.gelu(hT) * g_ref[:, cols]).astype(BF16)

    def accumulate_piece(w_ref, m, n_m):
        rws = slice(m * (d // n_m), (m + 1) * (d // n_m))
        o_ref[rws, :] += jnp.dot(vt_ref[rws, :], w_ref[...], preferred_element_type=F32)

    n_sp = 2 if tt >= 2 * LANES else 1

    @pl.when(j == 0)
    def _():
        o_ref[...] = jnp.zeros_like(o_ref)
        for r, c0 in gate_pieces:
            gate_piece(j, g_scr[0], r, c0)
        for p in range(n_sp):
            score_piece(g_scr[0], w_scr[0], p, n_sp)
        for r, c0 in gate_pieces:
            gate_piece(j + 1, g_scr[1], r, c0)

    for par in range(2):
        @pl.when(jnp.logical_and(jnp.logical_and(j > 0, j < n_tiles), j % 2 == par))
        def _():
            nxt = jnp.minimum(j + 1, n_tiles - 1)
            for m in range(n_g):
                accumulate_piece(w_scr[1 - par], m, n_g)
                gate_piece(nxt, g_scr[1 - par], *gate_pieces[m])
            for p in range(n_sp):
                score_piece(g_scr[par], w_scr[par], p, n_sp)

    @pl.when(j == n_tiles)
    def _():
        accumulate_piece(w_scr[(n_tiles - 1) % 2], 0, 1)


def peer_experts(h2d, u_all, vt_all, layer, s1, s2, tau, mz):
    N, d = h2d.shape
    n_exp = u_all.shape[1]
    n_heads, n_keys, _ = s2.shape
    tt = min(N, 512)
    te = 512
    rows = te // n_keys
    n_tiles = n_exp // te
    assert n_tiles % 2 == 0
    once = pl.Buffered(1)
    return pl.pallas_call(
        functools.partial(_peer_expert_kernel, n_heads=n_heads, n_keys=n_keys, rows=rows, n_tiles=n_tiles),
        grid=(N // tt, n_tiles + 1),
        in_specs=[pl.BlockSpec((tt, d), lambda i, j: (i, 0), pipeline_mode=once),
                  pl.BlockSpec((None, te, d), lambda i, j: (layer, jnp.minimum(j, n_tiles - 1), 0)),
                  pl.BlockSpec((None, d, te), lambda i, j: (layer, 0, jnp.maximum(j - 1, 0))),
                  pl.BlockSpec((n_keys, n_heads, tt), lambda i, j: (0, 0, i), pipeline_mode=once),
                  pl.BlockSpec((n_heads, n_keys, tt), lambda i, j: (0, 0, i), pipeline_mode=once),
                  pl.BlockSpec((n_heads, tt), lambda i, j: (0, i), pipeline_mode=once),
                  pl.BlockSpec((n_heads, tt), lambda i, j: (0, i), pipeline_mode=once)],
        out_specs=pl.BlockSpec((d, tt), lambda i, j: (0, i)),
        out_shape=jax.ShapeDtypeStruct((d, N), F32),
        scratch_shapes=[pltpu.VMEM((te, tt), BF16)] * 2 + [pltpu.VMEM((te, tt), F32)] * 2,
        compiler_params=_params(("parallel", "arbitrary"), 56),
        name="peer_experts",
    )(h2d, u_all, vt_all, s1, s2, tau, mz)


def peer_ffn(h, wq_t, subkeys_b, u_all, vt_all, layer):
    B, T, d = h.shape
    h2d = h.reshape(B * T, d)
    s1, s2, tau, mz = peer_scores(h2d, wq_t, subkeys_b)
    return peer_experts(h2d, u_all, vt_all, layer, s1, s2, tau, mz)


def _pad_cols(w, n):
    return jnp.pad(w, ((0, 0), (0, n - w.shape[1])))


def kernel(x_prompt, x_sample, c_prompt, c_sample, state_rwkv_shift, state_rwkv, state_ret, cache_fox_k, cache_fox_v, cache_fox_logf, ada_w, ada_b, norm_g, final_g, w_in_even, w_out_even, sgu_norm_g, sgu_w, sgu_b, rwkv_mu, rwkv_w0, rwkv_w_up, rwkv_a0, rwkv_a_up, rwkv_g_up, rwkv_k_k, rwkv_k_a, rwkv_r_k, rwkv_ln_w, rwkv_ln_b, w_in_odd, w_out_odd, ret_gn_g, fox_b_f, peer_wq, peer_subkeys, peer_u, peer_v):
    depth = ada_w.shape[0]
    d = x_prompt.shape[-1]
    d_a = d_b = d // 2
    h_b = d_b // DH_B
    h_c = d // (2 * DV_C)
    h_d = d // (2 * DH_D)
    p_b = rwkv_mu.shape[1]
    n_bp, n_bs = x_prompt.shape[0], x_sample.shape[0]
    past_len = cache_fox_k.shape[2]

    n_rows = _round_up(n_bp + n_bs, 16)
    c_all = jnp.pad(jnp.concatenate([c_prompt, c_sample], axis=0), ((0, n_rows - n_bp - n_bs), (0, 0)))
    mod4 = ada_modulation(c_all, ada_w, ada_b).reshape(depth, n_rows, 1, 6 * d)

    w_in_e = [_pad_cols(w_in_even[j], 2 * d_a + 3 * d_b + TAIL_BLOCK).astype(BF16) for j in range(w_in_even.shape[0])]
    w_in_o = [_pad_cols(w_in_odd[j], _round_up(w_in_odd.shape[2], 512)).astype(BF16) for j in range(w_in_odd.shape[0])]
    w_out_e = [w_out_even[j].astype(BF16) for j in range(w_out_even.shape[0])]
    w_out_o = [w_out_odd[j].astype(BF16) for j in range(w_out_odd.shape[0])]
    n_hp = peer_subkeys.shape[1]
    wq_t = [peer_wq[i].T.astype(BF16) for i in range(depth)]
    subk = [peer_subkeys[i].reshape(2 * n_hp, peer_subkeys.shape[3], peer_subkeys.shape[4]).astype(BF16)
            for i in range(depth)]
    u_all = peer_u.astype(BF16)
    vt_all = jnp.swapaxes(peer_v, 1, 2).astype(BF16)

    def run(x, row0, pos0, shift0, rwkv0, ret0, fox_past):
        B, T, _ = x.shape
        pos = pos0 + jnp.arange(T)
        v_rows, shifts, rwkvs, rets, ks, vs, lfs = [], [], [], [], [], [], []
        pending = None
        for i in range(depth):
            j = i // 2
            res = None if pending is None else (pending[0], pending[1], mod4, pending[3], row0, pending[2])
            x_new, h = resnorm(x, norm_g[i, 0], res=res, mod=(mod4, i, row0, 1, 0))
            x = x if x_new is None else x_new
            h2d = h.reshape(B * T, d)
            if i % 2 == 0:
                p = matmul(h2d, w_in_e[j]).reshape(B, T, -1)
                out_a, vn = spatial_gating(p, d_a, sgu_norm_g[j], sgu_w[j], sgu_b[j])
                r, w, k, v, a, g = rwkv_prep(p, d_b, 2 * d_a, shift0[j], rwkv_mu[j], rwkv_w0[j], rwkv_w_up[j],
                                             rwkv_a0[j], rwkv_a_up[j], rwkv_g_up[j])
                yield "relayout"
                yb, S = rwkv_scan(r, w, k, v, a, rwkv0[j], rwkv_k_k[j], rwkv_k_a[j], rwkv_r_k[j], rwkv_ln_w[j],
                                  rwkv_ln_b[j])
                mix = gate_concat(out_a, yb, g)
                y = matmul(mix.reshape(B * T, d), w_out_e[j]).reshape(B, T, d)
                v_rows.append(vn)
                shifts.append(p[:, -1:, 2 * d_a:2 * d_a + p_b])
                rwkvs.append(S)
            else:
                p = matmul(h2d, w_in_o[j]).reshape(B, T, -1)
                c_dq = 2 * h_c * DK_C + 2 * h_c * DV_C
                c_dk, c_dv, c_df = c_dq + h_d * DH_D, c_dq + 2 * h_d * DH_D, c_dq + 3 * h_d * DH_D
                out_c, R = retention(p, h_c, pos, ret0[j], ret_gn_g[j])
                logf = jax.nn.log_sigmoid(p[..., c_df:c_df + h_d] + fox_b_f[j])
                kd = p[..., c_dk:c_dk + h_d * DH_D]
                vd = p[..., c_dv:c_dv + h_d * DH_D]
                if fox_past is None:
                    F_all = jnp.cumsum(logf, axis=1)
                    od = fox_attention(p, c_dq // DH_D, p, c_dk // DH_D, p, c_dv // DH_D, F_all, h_d, T)
                else:
                    k_past, v_past, lf_past = fox_past[j]
                    k_all = jnp.concatenate([k_past.reshape(B, past_len, -1), kd], axis=1)
                    v_all = jnp.concatenate([v_past.reshape(B, past_len, -1), vd], axis=1)
                    F_all = jnp.cumsum(jnp.concatenate([lf_past, logf], axis=1), axis=1)
                    od = fox_attention(p, c_dq // DH_D, k_all, 0, v_all, 0, F_all, h_d, T)
                mix = jnp.concatenate([out_c, od], axis=-1)
                y = matmul(mix.reshape(B * T, d), w_out_o[j]).reshape(B, T, d)
                rets.append(R)
                ks.append(kd.reshape(B, T, h_d, DH_D))
                vs.append(vd.reshape(B, T, h_d, DH_D))
                lfs.append(logf)
            x, h = resnorm(x, norm_g[i, 1], res=(y, False, mod4, i, row0, 2), mod=(mod4, i, row0, 4, 3))
            y_t = peer_ffn(h, wq_t[i], subk[i], u_all, vt_all, i)
            if T % LANES == 0:
                pending = (y_t, True, 5, i)
            else:
                pending = (y_t.T.reshape(B, T, d), False, 5, i)
            yield "layer"
        _, out = resnorm(x, final_g, res=(pending[0], pending[1], mod4, pending[3], row0, pending[2]),
                         out_dtype=F32)
        return out, v_rows, shifts, rwkvs, rets, ks, vs, lfs

    n_even, n_odd = w_in_even.shape[0], w_in_odd.shape[0]
    run_p = run(
        x_prompt, 0, 0,
        [jnp.zeros((n_bp, 1, p_b), F32)] * n_even,
        [jnp.zeros((n_bp, h_b, DH_B, DH_B), F32)] * n_even,
        [jnp.zeros((n_bp, h_c, DK_C, DV_C), F32)] * n_odd,
        None)
    run_s = run(
        x_sample, n_bp, past_len,
        [state_rwkv_shift[j] for j in range(n_even)],
        [state_rwkv[j] for j in range(n_even)],
        [state_ret[j] for j in range(n_odd)],
        [(cache_fox_k[j], cache_fox_v[j], cache_fox_logf[j]) for j in range(n_odd)])

    def finish_layer(gen):
        while next(gen) != "layer":
            pass

    def result(gen):
        try:
            next(gen)
        except StopIteration as stop:
            return stop.value
        raise AssertionError("run() yielded after its last layer")

    for _ in range(depth):
        prompt_mid_layer = next(run_p) == "relayout"
        finish_layer(run_s)
        if prompt_mid_layer:
            finish_layer(run_p)
    y_p, _, sh_p, rw_p, rt_p, k_p, v_p, lf_p = result(run_p)
    y_s, vr_s, sh_s, rw_s, rt_s, k_s, v_s, lf_s = result(run_s)
    return (y_p, y_s, jnp.stack(vr_s), jnp.stack(sh_p), jnp.stack(sh_s), jnp.stack(rw_p), jnp.stack(rw_s),
            jnp.stack(rt_p), jnp.stack(rt_s), jnp.stack(k_p), jnp.stack(k_s), jnp.stack(v_p), jnp.stack(v_s),
            jnp.stack(lf_p), jnp.stack(lf_s))
```

```python
import functools
import math

import jax
import jax.numpy as jnp
from jax import lax
from jax.experimental import pallas as pl
from jax.experimental.pallas import tpu as pltpu

F32 = jnp.float32
BF16 = jnp.bfloat16

EPS = 1e-6
RWKV_GN_EPS = 64e-5
ROPE_BASE = 10000.0
RET_CHUNK = 64
MLP_CHUNK = 128
N_GROUPS_A = 8
DH_B = 64
DK_C = 128
DV_C = 256
DH_D = 128
TOPK_P = 16

LANES = 128
TAIL_BLOCK = 512
MIB = 1024 * 1024


def _round_up(n, m):
    return (n + m - 1) // m * m


def _params(sem, vmem_mib):
    return pltpu.CompilerParams(dimension_semantics=sem, vmem_limit_bytes=int(vmem_mib * MIB))


def _mm_kernel(a_ref, b_ref, o_ref):
    o_ref[...] = jnp.dot(a_ref[...], b_ref[...], preferred_element_type=F32).astype(o_ref.dtype)


def matmul(a, b, out_dtype=F32):
    M, K = a.shape
    N = b.shape[1]
    tm = min(M, 1024)
    tn = 512
    assert M % tm == 0 and N % tn == 0, (M, N)
    return pl.pallas_call(
        _mm_kernel,
        grid=(M // tm, N // tn),
        in_specs=[pl.BlockSpec((tm, K), lambda i, j: (i, 0)),
                  pl.BlockSpec((K, tn), lambda i, j: (0, j))],
        out_specs=pl.BlockSpec((tm, tn), lambda i, j: (i, j)),
        out_shape=jax.ShapeDtypeStruct((M, N), out_dtype),
        compiler_params=_params(("parallel", "parallel"), 48),
        name="matmul",
    )(a, b)


def _ada_kernel(c_ref, w_ref, b_ref, o_ref):
    c = c_ref[...]
    a = (c * jax.nn.sigmoid(c)).astype(BF16)
    o_ref[...] = jnp.dot(a, w_ref[...].astype(BF16), preferred_element_type=F32) + b_ref[...]


def ada_modulation(c_all, ada_w, ada_b):
    depth, d, n6 = ada_w.shape
    R = c_all.shape[0]
    tn = 512
    return pl.pallas_call(
        _ada_kernel,
        grid=(depth, n6 // tn),
        in_specs=[pl.BlockSpec((R, d), lambda i, j: (0, 0)),
                  pl.BlockSpec((None, d, tn), lambda i, j: (i, 0, j)),
                  pl.BlockSpec((None, 1, tn), lambda i, j: (i, 0, j))],
        out_specs=pl.BlockSpec((None, R, tn), lambda i, j: (i, 0, j)),
        out_shape=jax.ShapeDtypeStruct((depth, R, n6), F32),
        compiler_params=_params(("parallel", "parallel"), 40),
        name="ada_modulation",
    )(c_all, ada_w, ada_b.reshape(depth, 1, n6))


def _resnorm_kernel(*refs, has_res, y_transposed, has_mod):
    it = iter(refs)
    x_ref = next(it)
    if has_res:
        y_ref = next(it)
        gate_ref = next(it)
    g_ref = next(it)
    if has_mod:
        sc_ref = next(it)
        sh_ref = next(it)
    if has_res:
        xo_ref = next(it)
    h_ref = next(it)
    x = x_ref[...]
    if has_res:
        y = y_ref[...]
        if y_transposed:
            y = y.T
        x = x + gate_ref[...] * y
        xo_ref[...] = x
    xn = x * lax.rsqrt(jnp.mean(x * x, axis=-1, keepdims=True) + EPS) * g_ref[...]
    if has_mod:
        xn = xn * (1.0 + sc_ref[...]) + sh_ref[...]
    h_ref[...] = xn.astype(h_ref.dtype)


def resnorm(x, norm_g, *, res=None, mod=None, out_dtype=BF16):
    B, T, d = x.shape
    tt = min(T, 256)
    nt = T // tt
    row = pl.BlockSpec((None, tt, d), lambda b, i: (b, i, 0))
    args, specs = [x], [row]
    if res is not None:
        y, y_transposed, mod4, layer, row0, gk = res
        args.append(y)
        if y_transposed:
            specs.append(pl.BlockSpec((d, tt), lambda b, i: (0, b * nt + i)))
        else:
            specs.append(row)
        args.append(mod4)
        specs.append(pl.BlockSpec((None, None, 1, d), lambda b, i: (layer, row0 + b, 0, gk)))
    else:
        y_transposed = False
    args.append(norm_g.reshape(1, d))
    specs.append(pl.BlockSpec((1, d), lambda b, i: (0, 0)))
    if mod is not None:
        mod4m, layer_m, row0_m, sck, shk = mod
        args += [mod4m, mod4m]
        specs.append(pl.BlockSpec((None, None, 1, d), lambda b, i: (layer_m, row0_m + b, 0, sck)))
        specs.append(pl.BlockSpec((None, None, 1, d), lambda b, i: (layer_m, row0_m + b, 0, shk)))
    out_shape, out_specs = [], []
    if res is not None:
        out_shape.append(jax.ShapeDtypeStruct((B, T, d), F32))
        out_specs.append(row)
    out_shape.append(jax.ShapeDtypeStruct((B, T, d), out_dtype))
    out_specs.append(row)
    outs = pl.pallas_call(
        functools.partial(_resnorm_kernel, has_res=res is not None, y_transposed=y_transposed,
                          has_mod=mod is not None),
        grid=(B, nt),
        in_specs=specs,
        out_specs=out_specs,
        out_shape=out_shape,
        compiler_params=_params(("parallel", "parallel"), 48),
        name="resnorm",
    )(*args)
    if res is not None:
        return outs[0], outs[1]
    return None, outs[0]


def _sgu_kernel(pu_ref, pv_ref, ng_ref, w_ref, b_ref, oa_ref, vn_ref, *, groups):
    u = jax.nn.gelu(pu_ref[...])
    v = jax.nn.gelu(pv_ref[...])
    dg = v.shape[1] // groups
    for g in range(groups):
        sl = slice(g * dg, (g + 1) * dg)
        vg = v[:, sl]
        mean = jnp.mean(vg, axis=-1, keepdims=True)
        cen = vg - mean
        var = jnp.mean(cen * cen, axis=-1, keepdims=True)
        vn = cen * lax.rsqrt(var + EPS) * ng_ref[:, sl]
        vn_ref[:, sl] = vn
        s = jnp.dot(w_ref[g], vn.astype(BF16), preferred_element_type=F32) + b_ref[:, sl]
        oa_ref[:, sl] = (u[:, sl] * s).astype(oa_ref.dtype)


def spatial_gating(p, d_a, norm_g, w_s, b_s):
    B, T, _ = p.shape
    L = min(T, MLP_CHUNK)
    G = w_s.shape[0]
    dg = d_a // G
    w_tril = jnp.tril(w_s)[:, :L, :L].astype(BF16)
    b_full = jnp.repeat(b_s.T[:L], dg, axis=1)
    blk = lambda c: pl.BlockSpec((None, L, d_a), lambda b, i: (b, i, c))
    return pl.pallas_call(
        functools.partial(_sgu_kernel, groups=G),
        grid=(B, T // L),
        in_specs=[blk(0), blk(1),
                  pl.BlockSpec((1, d_a), lambda b, i: (0, 0)),
                  pl.BlockSpec((G, L, L), lambda b, i: (0, 0, 0)),
                  pl.BlockSpec((L, d_a), lambda b, i: (0, 0))],
        out_specs=[blk(0), blk(0)],
        out_shape=[jax.ShapeDtypeStruct((B, T, d_a), BF16), jax.ShapeDtypeStruct((B, T, d_a), F32)],
        compiler_params=_params(("parallel", "parallel"), 32),
        name="spatial_gating",
    )(p, p, norm_g.reshape(1, d_a), w_tril, b_full)


def _shifted(x_ref, halo_ref, state_ref, mu_ref):
    x = x_ref[...]
    first = jnp.where(pl.program_id(1) == 0, state_ref[...], halo_ref[7:8, :])
    rows = lax.broadcasted_iota(jnp.int32, x.shape, 0)
    prev = jnp.where(rows == 0, first, pltpu.roll(x, 1, axis=0))
    return x + (prev - x) * mu_ref[...]


def _rwkv_prep_kernel(r_ref, k_ref, v_ref, t_ref, rh_ref, kh_ref, vh_ref, th_ref,
                      sr_ref, sk_ref, sv_ref, st_ref, mr_ref, mk_ref, mv_ref, mt_ref,
                      w0_ref, wup_ref, a0_ref, aup_ref, gup_ref,
                      ro_ref, wo_ref, ko_ref, vo_ref, ao_ref, go_ref, *, lw, la, lg):
    r = _shifted(r_ref, rh_ref, sr_ref, mr_ref)
    k = _shifted(k_ref, kh_ref, sk_ref, mk_ref)
    v = _shifted(v_ref, vh_ref, sv_ref, mv_ref)
    t = _shifted(t_ref, th_ref, st_ref, mt_ref)
    wd, ad, gd = t[:, :lw], t[:, lw:lw + la], t[:, lw + la:lw + la + lg]
    wl = w0_ref[...] + jnp.dot(jnp.tanh(wd).astype(BF16), wup_ref[...], preferred_element_type=F32)
    w_log = -jax.nn.softplus(-wl) - 0.5
    wo_ref[...] = jnp.exp(-jnp.exp(w_log))
    a = jax.nn.sigmoid(a0_ref[...] + jnp.dot(ad.astype(BF16), aup_ref[...], preferred_element_type=F32))
    go_ref[...] = jnp.dot(jax.nn.sigmoid(gd).astype(BF16), gup_ref[...], preferred_element_type=F32)
    ro_ref[...] = r
    vo_ref[...] = v
    ao_ref[...] = a
    ko_ref[...] = k


def rwkv_prep(p, d_b, col0, shift_state, mu, w0, w_up, a0, a_up, g_up):
    B, T, _ = p.shape
    lw, la, lg = w_up.shape[0], a_up.shape[0], g_up.shape[0]
    tt = min(T, 128)
    cb = col0 // d_b
    tb = (col0 + 3 * d_b) // TAIL_BLOCK
    assert col0 % d_b == 0 and (col0 + 3 * d_b) % TAIL_BLOCK == 0 and lw + la + lg <= TAIL_BLOCK
    pad = TAIL_BLOCK - (lw + la + lg)

    def seg(c, width):
        return pl.BlockSpec((None, tt, width), lambda b, i: (b, i, c))

    def halo(c, width):
        return pl.BlockSpec((None, 8, width), lambda b, i: (b, jnp.maximum(i * (tt // 8) - 1, 0), c))

    def per_b(width):
        return pl.BlockSpec((None, 1, width), lambda b, i: (b, 0, 0))

    def full(shape):
        return pl.BlockSpec(shape, lambda b, i: tuple(0 for _ in shape))

    st = shift_state
    s_parts = [st[..., :d_b], st[..., d_b:2 * d_b], st[..., 2 * d_b:3 * d_b],
               jnp.pad(st[..., 3 * d_b:], ((0, 0), (0, 0), (0, pad)))]
    m_parts = [mu[None, :d_b], mu[None, d_b:2 * d_b], mu[None, 2 * d_b:3 * d_b],
               jnp.pad(mu[None, 3 * d_b:], ((0, 0), (0, pad)))]
    out_row = pl.BlockSpec((None, tt, d_b), lambda b, i: (b, i, 0))
    outs = pl.pallas_call(
        functools.partial(_rwkv_prep_kernel, lw=lw, la=la, lg=lg),
        grid=(B, T // tt),
        in_specs=[seg(cb, d_b), seg(cb + 1, d_b), seg(cb + 2, d_b), seg(tb, TAIL_BLOCK),
                  halo(cb, d_b), halo(cb + 1, d_b), halo(cb + 2, d_b), halo(tb, TAIL_BLOCK),
                  per_b(d_b), per_b(d_b), per_b(d_b), per_b(TAIL_BLOCK),
                  full((1, d_b)), full((1, d_b)), full((1, d_b)), full((1, TAIL_BLOCK)),
                  full((1, d_b)), full((lw, d_b)), full((1, d_b)), full((la, d_b)), full((lg, d_b))],
        out_specs=[out_row] * 6,
        out_shape=[jax.ShapeDtypeStruct((B, T, d_b), F32)] * 6,
        compiler_params=_params(("parallel", "arbitrary"), 56),
        name="rwkv_prep",
    )(p, p, p, p, p, p, p, p, *s_parts, *m_parts,
      w0[None], w_up.astype(BF16), a0[None], a_up.astype(BF16), g_up.astype(BF16))
    return outs


def _rwkv_scan_kernel(r_ref, w_ref, k_ref, v_ref, a_ref, kk_ref, ka_ref, rk_ref, lnw_ref, lnb_ref, s0_ref,
                      y_ref, so_ref, s_scr, av_scr, bv_scr, k2_scr, *, tc, dh):
    @pl.when(pl.program_id(1) == 0)
    def _():
        s_scr[...] = s0_ref[...]

    def key_vectors(t):
        kr = k_ref[t] * kk_ref[...]
        kk = kr * lax.rsqrt(jnp.maximum(jnp.sum(kr * kr, axis=0, keepdims=True), 1e-24))
        return -kk, kk * a_ref[t]

    def step(t, sa):
        a = a_ref[t]
        k2 = k_ref[t] * (1.0 + (a - 1.0) * ka_ref[...])
        k2_scr[...] = k2
        av_next, bv_next = key_vectors(jnp.minimum(t + 1, tc - 1))
        av_scr[...] = av_next
        v = v_ref[t]
        y = jnp.zeros_like(v)
        sa_next = jnp.zeros_like(v)
        for k in range(dh):
            s_new = (s_scr[k] * w_ref[t, k:k + 1, :] + sa * bv_scr[k:k + 1, :] + v * k2_scr[k:k + 1, :])
            s_scr[k] = s_new
            y = y + s_new * r_ref[t, k:k + 1, :]
            sa_next = sa_next + s_new * av_scr[k:k + 1, :]
        bv_scr[...] = bv_next
        mean = jnp.mean(y, axis=0, keepdims=True)
        cen = y - mean
        var = jnp.mean(cen * cen, axis=0, keepdims=True)
        yn = cen * lax.rsqrt(var + RWKV_GN_EPS) * lnw_ref[...] + lnb_ref[...]
        bonus = jnp.sum(r_ref[t] * k2 * rk_ref[...], axis=0, keepdims=True) * v
        y_ref[t] = yn + bonus
        return sa_next

    av0, bv0 = key_vectors(0)
    av_scr[...] = av0
    bv_scr[...] = bv0
    sa0 = jnp.zeros((dh, LANES), F32)
    for k in range(dh):
        sa0 = sa0 + s_scr[k] * av_scr[k:k + 1, :]
    lax.fori_loop(0, tc, step, sa0)

    @pl.when(pl.program_id(1) == pl.num_programs(1) - 1)
    def _():
        so_ref[...] = s_scr[...]


def rwkv_scan(r, w, k, v, a, S0, k_k, k_a, r_k, ln_w, ln_b):
    B, T, d_b = r.shape
    H = d_b // DH_B
    BH = B * H
    BHp = _round_up(BH, LANES)

    def to_lanes(x):
        x = x.reshape(B, T, H, DH_B).transpose(1, 3, 0, 2).reshape(T, DH_B, BH)
        return jnp.pad(x, ((0, 0), (0, 0), (0, BHp - BH)))

    def param_lanes(x):
        x = jnp.tile(x.T, (1, B))
        return jnp.pad(x, ((0, 0), (0, BHp - BH)))

    s0 = S0.transpose(3, 2, 0, 1).reshape(DH_B, DH_B, BH)
    s0 = jnp.pad(s0, ((0, 0), (0, 0), (0, BHp - BH)))
    tc = min(T, 64)
    seq = pl.BlockSpec((tc, DH_B, LANES), lambda l, i: (i, 0, l))
    par = pl.BlockSpec((DH_B, LANES), lambda l, i: (0, l))
    st = pl.BlockSpec((DH_B, DH_B, LANES), lambda l, i: (0, 0, l))
    y, s_fin = pl.pallas_call(
        functools.partial(_rwkv_scan_kernel, tc=tc, dh=DH_B),
        grid=(BHp // LANES, T // tc),
        in_specs=[seq] * 5 + [par] * 5 + [st],
        out_specs=[seq, st],
        out_shape=[jax.ShapeDtypeStruct((T, DH_B, BHp), F32), jax.ShapeDtypeStruct((DH_B, DH_B, BHp), F32)],
        scratch_shapes=[pltpu.VMEM((DH_B, DH_B, LANES), F32)] + [pltpu.VMEM((DH_B, LANES), F32)] * 3,
        compiler_params=_params(("parallel", "arbitrary"), 40),
        name="rwkv_scan",
    )(to_lanes(r), to_lanes(w), to_lanes(k), to_lanes(v), to_lanes(a),
      param_lanes(k_k.reshape(H, DH_B)), param_lanes(k_a.reshape(H, DH_B)), param_lanes(r_k),
      param_lanes(ln_w.reshape(H, DH_B)), param_lanes(ln_b.reshape(H, DH_B)), s0)
    y = y[:, :, :BH].reshape(T, DH_B, B, H).transpose(2, 0, 3, 1).reshape(B, T, d_b)
    s_fin = s_fin[:, :, :BH].reshape(DH_B, DH_B, B, H).transpose(2, 3, 1, 0)
    return y, s_fin


def _gate_concat_kernel(oa_ref, yb_ref, g_ref, o_ref, *, d_a):
    o_ref[:, :d_a] = oa_ref[...]
    o_ref[:, d_a:] = (yb_ref[...] * g_ref[...]).astype(o_ref.dtype)


def gate_concat(out_a, yb, g):
    B, T, d_a = out_a.shape
    d_b = yb.shape[-1]
    tt = min(T, 256)
    blk = lambda w: pl.BlockSpec((None, tt, w), lambda b, i: (b, i, 0))
    return pl.pallas_call(
        functools.partial(_gate_concat_kernel, d_a=d_a),
        grid=(B, T // tt),
        in_specs=[blk(d_a), blk(d_b), blk(d_b)],
        out_specs=blk(d_a + d_b),
        out_shape=jax.ShapeDtypeStruct((B, T, d_a + d_b), BF16),
        compiler_params=_params(("parallel", "parallel"), 32),
        name="gate_concat",
    )(out_a, yb, g)


def _rope(x, cos2, sin2):
    return x * cos2 + pltpu.roll(x, x.shape[-1] // 2, axis=1) * sin2


def _retention_kernel(q_ref, k_ref, v_ref, g_ref, cos_ref, sin_ref, dm_ref, kd_ref, qd_ref, gl_ref, gn_ref,
                      r0_ref, o_ref, ro_ref, r_scr, *, L, n_chunks):
    @pl.when(pl.program_id(2) == 0)
    def _():
        r_scr[...] = r0_ref[...]

    scale = DK_C ** -0.5
    for c in range(n_chunks):
        rows = slice(c * L, (c + 1) * L)
        cos2, sin2 = cos_ref[rows, :], sin_ref[rows, :]
        q = _rope(q_ref[rows, :], cos2, sin2)
        k = _rope(k_ref[rows, :], cos2, sin2) * scale
        v = v_ref[rows, :].astype(BF16)
        att = lax.dot_general(q.astype(BF16), k.astype(BF16), (((1,), (1,)), ((), ())),
                              preferred_element_type=F32) * dm_ref[...]
        R = r_scr[...]
        o = jnp.dot(att.astype(BF16), v, preferred_element_type=F32)
        o = o + jnp.dot((q * qd_ref[...]).astype(BF16), R.astype(BF16), preferred_element_type=F32)
        kd_t = (k * kd_ref[...]).T.astype(BF16)
        r_scr[...] = R * gl_ref[...] + jnp.dot(kd_t, v, preferred_element_type=F32)
        mean = jnp.mean(o, axis=-1, keepdims=True)
        cen = o - mean
        var = jnp.mean(cen * cen, axis=-1, keepdims=True)
        on = cen * lax.rsqrt(var + EPS) * gn_ref[...]
        gate = g_ref[rows, :]
        o_ref[rows, :] = (gate * jax.nn.sigmoid(gate) * on).astype(o_ref.dtype)

    @pl.when(pl.program_id(2) == pl.num_programs(2) - 1)
    def _():
        ro_ref[...] = r_scr[...]


def retention(p, h_c, pos, R0, gn_g):
    B, T, _ = p.shape
    L = math.gcd(T, RET_CHUNK)
    tr = min(T, 512)
    n_chunks = tr // L
    half = DK_C // 2
    inv = ROPE_BASE ** (-jnp.arange(half, dtype=F32) / half)
    ang = pos.astype(F32)[:, None] * inv[None, :]
    cos2 = jnp.concatenate([jnp.cos(ang), jnp.cos(ang)], axis=-1)
    sin2 = jnp.concatenate([-jnp.sin(ang), jnp.sin(ang)], axis=-1)
    log_g = jnp.log1p(-jnp.exp2(-5.0 - jnp.arange(h_c, dtype=F32)))
    idx = jnp.arange(L, dtype=F32)
    rel = idx[:, None] - idx[None, :]
    dmat = jnp.where(rel >= 0, jnp.exp(jnp.maximum(rel, 0.0) * log_g[:, None, None]), 0.0)
    k_dec = jnp.exp((L - 1 - idx)[None, :, None] * log_g[:, None, None])
    q_dec = jnp.exp((idx + 1)[None, :, None] * log_g[:, None, None])
    k_dec = jnp.broadcast_to(k_dec, (h_c, L, DK_C))
    q_dec = jnp.broadcast_to(q_dec, (h_c, L, DK_C))
    g_l = jnp.broadcast_to(jnp.exp(L * log_g)[:, None, None], (h_c, 1, DV_C))

    def col(width, c0):
        return pl.BlockSpec((None, tr, width), lambda b, h, i: (b, i, c0 + h))

    def head(shape):
        return pl.BlockSpec((None,) + shape, lambda b, h, i: (h,) + tuple(0 for _ in shape))

    trig = pl.BlockSpec((tr, DK_C), lambda b, h, i: (i, 0))
    state = pl.BlockSpec((None, None, DK_C, DV_C), lambda b, h, i: (b, h, 0, 0))
    return pl.pallas_call(
        functools.partial(_retention_kernel, L=L, n_chunks=n_chunks),
        grid=(B, h_c, T // tr),
        in_specs=[col(DK_C, 0), col(DK_C, h_c), col(DV_C, h_c), col(DV_C, 2 * h_c), trig, trig,
                  head((L, L)), head((L, DK_C)), head((L, DK_C)), head((1, DV_C)),
                  pl.BlockSpec((1, DV_C), lambda b, h, i: (0, h)), state],
        out_specs=[pl.BlockSpec((None, tr, DV_C), lambda b, h, i: (b, i, h)), state],
        out_shape=[jax.ShapeDtypeStruct((B, T, h_c * DV_C), BF16),
                   jax.ShapeDtypeStruct((B, h_c, DK_C, DV_C), F32)],
        scratch_shapes=[pltpu.VMEM((DK_C, DV_C), F32)],
        compiler_params=_params(("parallel", "parallel", "arbitrary"), 32),
        name="retention",
    )(p, p, p, p, cos2, sin2, dmat, k_dec, q_dec, g_l, gn_g.reshape(1, h_c * DV_C), R0)


def _fox_kernel(q_ref, k_ref, v_ref, fq_ref, fk_ref, o_ref, kb_scr, vt_scr, acc_scr, *, tq, tkc, past):
    qi = pl.program_id(2)
    n_kv = k_ref.shape[0] // tkc

    @pl.when(qi == 0)
    def _():
        for c in range(n_kv):
            rows = slice(c * tkc, (c + 1) * tkc)
            kb_scr[rows, :] = k_ref[rows, :].astype(BF16)
            vt_scr[:, rows] = v_ref[rows, :].T.astype(BF16)

    q = q_ref[...].astype(BF16)
    fq = fq_ref[...]
    scale = DH_D ** -0.5
    n_rep = max(tq // LANES, 1)

    def scores(c, masked):
        r0 = 0 if n_kv == 1 else pl.multiple_of(c * tkc, tkc)
        s = lax.dot_general(kb_scr[pl.ds(r0, tkc), :], q, (((1,), (1,)), ((), ())), preferred_element_type=F32)
        fk = fk_ref[pl.ds(r0, tkc), :]
        fk = jnp.concatenate([fk] * n_rep, axis=1) if tq >= LANES else fk[:, :tq]
        s = s * scale + (fq - fk)
        if masked:
            kpos = r0 + lax.broadcasted_iota(jnp.int32, s.shape, 0)
            qpos = past + qi * tq + lax.broadcasted_iota(jnp.int32, s.shape, 1)
            s = jnp.where(kpos <= qpos, s, -jnp.inf)
        return r0, s

    def chunks(cs, carry, masked):
        m_old, l_old = carry
        parts = [scores(c, masked) for c in cs]
        m_new = m_old
        for _, s in parts:
            m_new = jnp.maximum(m_new, jnp.max(s, axis=0, keepdims=True))
        alpha = jnp.exp(m_old - m_new)
        l_new = alpha * l_old
        upd = alpha * acc_scr[...]
        for r0, s in parts:
            p = jnp.exp(s - m_new)
            l_new = l_new + jnp.sum(p, axis=0, keepdims=True)
            upd = upd + jnp.dot(vt_scr[:, pl.ds(r0, tkc)], p.astype(BF16), preferred_element_type=F32)
        acc_scr[...] = upd
        return m_new, l_new

    def sweep(lo, hi, carry, masked):
        for width in (4, 2, 1):
            n_grp = (hi - lo) // width
            carry = lax.fori_loop(
                0, n_grp, lambda i, cr, lo=lo, width=width: chunks(tuple(lo + width * i + u for u in range(width)),
                                                                    cr, masked), carry)
            lo = lo + width * n_grp
        return carry

    acc_scr[...] = jnp.zeros_like(acc_scr)
    carry = (jnp.full((1, tq), -jnp.inf, F32), jnp.zeros((1, tq), F32))
    n_full = (past + qi * tq + 1) // tkc
    n_need = (past + qi * tq + tq - 1) // tkc + 1
    carry = sweep(0, n_full, carry, False)
    m_fin, l_fin = sweep(n_full, n_need, carry, True)
    o_ref[...] = (acc_scr[...] / l_fin).T.astype(o_ref.dtype)


def fox_attention(q_src, q_col, k_src, k_col, v_src, v_col, F_all, n_heads, T):
    B = q_src.shape[0]
    Tk = k_src.shape[1]
    past = Tk - T
    tq = min(T, 512)
    tkc = 256 if Tk % 256 == 0 else Tk
    assert T % tq == 0
    f_heads = F_all.transpose(0, 2, 1)
    f_row = f_heads[:, :, None, past:]
    f_rep = jnp.broadcast_to(f_heads[..., None], (B, n_heads, Tk, LANES))

    def kv(col0):
        return pl.BlockSpec((None, Tk, DH_D), lambda b, h, i: (b, 0, col0 + h))

    return pl.pallas_call(
        functools.partial(_fox_kernel, tq=tq, tkc=tkc, past=past),
        grid=(B, n_heads, T // tq),
        in_specs=[pl.BlockSpec((None, tq, DH_D), lambda b, h, i: (b, i, q_col + h)),
                  kv(k_col), kv(v_col),
                  pl.BlockSpec((None, None, 1, tq), lambda b, h, i: (b, h, 0, i)),
                  pl.BlockSpec((None, None, Tk, LANES), lambda b, h, i: (b, h, 0, 0))],
        out_specs=pl.BlockSpec((None, tq, DH_D), lambda b, h, i: (b, i, h)),
        out_shape=jax.ShapeDtypeStruct((B, T, n_heads * DH_D), BF16),
        scratch_shapes=[pltpu.VMEM((Tk, DH_D), BF16), pltpu.VMEM((DH_D, Tk), BF16), pltpu.VMEM((DH_D, tq), F32)],
        compiler_params=_params(("parallel", "parallel", "arbitrary"), 40),
        name="fox_attention",
    )(q_src, k_src, v_src, f_row, f_rep)


SUBLANES = 8


def _bitonic_merge_desc(v):
    v = list(v)
    n = len(v)
    step = n // 2
    while step >= 1:
        for i in range(n):
            if i & step == 0:
                a, b = v[i], v[i + step]
                v[i], v[i + step] = jnp.maximum(a, b), jnp.minimum(a, b)
        step //= 2
    return v


def _bitonic_sort_desc(v):
    v = list(v)
    n = len(v)
    size = 2
    while size <= n:
        step = size // 2
        while step >= 1:
            for i in range(n):
                if i & step == 0:
                    a, b = v[i], v[i + step]
                    hi, lo = jnp.maximum(a, b), jnp.minimum(a, b)
                    v[i], v[i + step] = (hi, lo) if i & size == 0 else (lo, hi)
            step //= 2
        size *= 2
    return v


def _merge_top(a, b):
    n = len(a)
    c = [jnp.maximum(a[i], b[n - 1 - i]) if n - 1 - i < len(b) else a[i] for i in range(n)]
    return _bitonic_merge_desc(c)


def _top_of_keys(s):
    slots = [s[SUBLANES * j:SUBLANES * (j + 1), :] for j in range(s.shape[0] // SUBLANES)]
    assert len(slots) == TOPK_P
    v = _bitonic_sort_desc(slots)
    shift = SUBLANES // 2
    while shift >= 1:
        partner = [pltpu.roll(x, shift, axis=0) for x in v]
        v = _merge_top(v, partner)
        shift //= 2
    return v


def _peer_score_kernel(h_ref, wq_ref, sk_ref, s1_ref, s2_ref, tau_ref, mz_ref, *, n_heads, n_keys):
    qT = lax.dot_general(wq_ref[...], h_ref[...], (((1,), (1,)), ((), ())), preferred_element_type=F32)
    dq = qT.shape[0] // (2 * n_heads)
    t = qT.shape[1]
    head_row = lax.broadcasted_iota(jnp.int32, (SUBLANES, t), 0)
    tops = [[jnp.zeros((SUBLANES, t), F32)] * TOPK_P for _ in range(2)]
    for hd in range(n_heads):
        for c in range(2):
            hc = 2 * hd + c
            q = qT[hc * dq:(hc + 1) * dq, :].astype(BF16)
            s = jnp.dot(sk_ref[hc], q, preferred_element_type=F32)
            if c == 0:
                s1_ref[:, hd, :] = s
            else:
                s2_ref[hd] = s
            top = _top_of_keys(s)
            tops[c] = [jnp.where(head_row == hd, top[i], tops[c][i]) for i in range(TOPK_P)]
    v1, v2 = tops
    best = [v1[0] + v2[b] for b in range(TOPK_P)]
    for a in range(1, TOPK_P):
        best = _merge_top(best, [v1[a] + v2[b] for b in range(TOPK_P // (a + 1))])
    z = jnp.zeros_like(best[0])
    for s_k in best:
        z = z + jnp.exp(s_k - best[0])
    tau_ref[...] = best[TOPK_P - 1]
    mz_ref[...] = best[0] + jnp.log(z)


def peer_scores(h2d, wq_t, subkeys_b):
    N, d = h2d.shape
    n_hc, n_keys, dq = subkeys_b.shape
    n_heads = n_hc // 2
    assert n_heads == SUBLANES and n_keys == TOPK_P * SUBLANES
    tt = min(N, 256)
    return pl.pallas_call(
        functools.partial(_peer_score_kernel, n_heads=n_heads, n_keys=n_keys),
        grid=(N // tt,),
        in_specs=[pl.BlockSpec((tt, d), lambda i: (i, 0)),
                  pl.BlockSpec(wq_t.shape, lambda i: (0, 0), pipeline_mode=pl.Buffered(1)),
                  pl.BlockSpec(subkeys_b.shape, lambda i: (0, 0, 0))],
        out_specs=[pl.BlockSpec((n_keys, n_heads, tt), lambda i: (0, 0, i)),
                   pl.BlockSpec((n_heads, n_keys, tt), lambda i: (0, 0, i)),
                   pl.BlockSpec((n_heads, tt), lambda i: (0, i)),
                   pl.BlockSpec((n_heads, tt), lambda i: (0, i))],
        out_shape=[jax.ShapeDtypeStruct((n_keys, n_heads, N), F32),
                   jax.ShapeDtypeStruct((n_heads, n_keys, N), F32),
                   jax.ShapeDtypeStruct((n_heads, N), F32),
                   jax.ShapeDtypeStruct((n_heads, N), F32)],
        compiler_params=_params(("parallel",), 48),
        name="peer_scores",
    )(h2d, wq_t, subkeys_b)


def _peer_expert_kernel(h_ref, u_ref, vt_ref, s1_ref, s2_ref, tau_ref, mz_ref, o_ref, w0_scr, w1_scr, g0_scr, g1_scr,
                        *, n_heads, n_keys, rows, n_tiles):
    j = pl.program_id(1)
    tt = h_ref.shape[0]
    d = o_ref.shape[0]
    cw = min(tt, LANES)
    w_scr = (w0_scr, w1_scr)
    g_scr = (g0_scr, g1_scr)
    gate_pieces = [(r, c0) for r in range(rows) for c0 in range(0, tt, cw)]
    n_g = len(gate_pieces)

    def gate_piece(tile, g_ref, r, c0):
        cols = slice(c0, c0 + cw)
        s1_heads = s1_ref[tile * rows + r]
        g = jnp.zeros((n_keys, cw), F32)
        for hd in range(n_heads):
            s = s1_heads[hd:hd + 1, cols] + s2_ref[hd, :, cols]
            e = jnp.exp(s - mz_ref[hd:hd + 1, cols])
            g = g + jnp.where(s >= tau_ref[hd:hd + 1, cols], e, 0.0)
        g_ref[r * n_keys:(r + 1) * n_keys, cols] = g

    def score_piece(g_ref, w_ref, piece, n_p):
        cols = slice(piece * (tt // n_p), (piece + 1) * (tt // n_p))
        hT = lax.dot_general(u_ref[...], h_ref[cols, :], (((1,), (1,)), ((), ())), preferred_element_type=F32)
        w_ref[:, cols] = (jax.nn.gelu(hT) * g_ref[:, cols]).astype(BF16)

    def accumulate_piece(w_ref, m, n_m):
        rws = slice(m * (d // n_m), (m + 1) * (d // n_m))
        o_ref[rws, :] += jnp.dot(vt_ref[rws, :], w_ref[...], preferred_element_type=F32)

    n_sp = 2 if tt >= 2 * LANES else 1

    @pl.when(j == 0)
    def _():
        o_ref[...] = jnp.zeros_like(o_ref)
        for r, c0 in gate_pieces:
            gate_piece(j, g_scr[0], r, c0)
        for p in range(n_sp):
            score_piece(g_scr[0], w_scr[0], p, n_sp)
        for r, c0 in gate_pieces:
            gate_piece(j + 1, g_scr[1], r, c0)

    for par in range(2):
        @pl.when(jnp.logical_and(jnp.logical_and(j > 0, j < n_tiles), j % 2 == par))
        def _():
            nxt = jnp.minimum(j + 1, n_tiles - 1)
            for m in range(n_g):
                accumulate_piece(w_scr[1 - par], m, n_g)
                gate_piece(nxt, g_scr[1 - par], *gate_pieces[m])
            for p in range(n_sp):
                score_piece(g_scr[par], w_scr[par], p, n_sp)

    @pl.when(j == n_tiles)
    def _():
        accumulate_piece(w_scr[(n_tiles - 1) % 2], 0, 1)


def peer_experts(h2d, u_all, vt_all, layer, s1, s2, tau, mz):
    N, d = h2d.shape
    n_exp = u_all.shape[1]
    n_heads, n_keys, _ = s2.shape
    tt = min(N, 512)
    te = 512
    rows = te // n_keys
    n_tiles = n_exp // te
    assert n_tiles % 2 == 0
    once = pl.Buffered(1)
    return pl.pallas_call(
        functools.partial(_peer_expert_kernel, n_heads=n_heads, n_keys=n_keys, rows=rows, n_tiles=n_tiles),
        grid=(N // tt, n_tiles + 1),
        in_specs=[pl.BlockSpec((tt, d), lambda i, j: (i, 0), pipeline_mode=once),
                  pl.BlockSpec((None, te, d), lambda i, j: (layer, jnp.minimum(j, n_tiles - 1), 0)),
                  pl.BlockSpec((None, d, te), lambda i, j: (layer, 0, jnp.maximum(j - 1, 0))),
                  pl.BlockSpec((n_keys, n_heads, tt), lambda i, j: (0, 0, i), pipeline_mode=once),
                  pl.BlockSpec((n_heads, n_keys, tt), lambda i, j: (0, 0, i), pipeline_mode=once),
                  pl.BlockSpec((n_heads, tt), lambda i, j: (0, i), pipeline_mode=once),
                  pl.BlockSpec((n_heads, tt), lambda i, j: (0, i), pipeline_mode=once)],
        out_specs=pl.BlockSpec((d, tt), lambda i, j: (0, i)),
        out_shape=jax.ShapeDtypeStruct((d, N), F32),
        scratch_shapes=[pltpu.VMEM((te, tt), BF16)] * 2 + [pltpu.VMEM((te, tt), F32)] * 2,
        compiler_params=_params(("parallel", "arbitrary"), 56),
        name="peer_experts",
    )(h2d, u_all, vt_all, s1, s2, tau, mz)


def peer_ffn(h, wq_t, subkeys_b, u_all, vt_all, layer):
    B, T, d = h.shape
    h2d = h.reshape(B * T, d)
    s1, s2, tau, mz = peer_scores(h2d, wq_t, subkeys_b)
    return peer_experts(h2d, u_all, vt_all, layer, s1, s2, tau, mz)


def _pad_cols(w, n):
    return jnp.pad(w, ((0, 0), (0, n - w.shape[1])))


def kernel(x_prompt, x_sample, c_prompt, c_sample, state_rwkv_shift, state_rwkv, state_ret, cache_fox_k, cache_fox_v, cache_fox_logf, ada_w, ada_b, norm_g, final_g, w_in_even, w_out_even, sgu_norm_g, sgu_w, sgu_b, rwkv_mu, rwkv_w0, rwkv_w_up, rwkv_a0, rwkv_a_up, rwkv_g_up, rwkv_k_k, rwkv_k_a, rwkv_r_k, rwkv_ln_w, rwkv_ln_b, w_in_odd, w_out_odd, ret_gn_g, fox_b_f, peer_wq, peer_subkeys, peer_u, peer_v):
    depth = ada_w.shape[0]
    d = x_prompt.shape[-1]
    d_a = d_b = d // 2
    h_b = d_b // DH_B
    h_c = d // (2 * DV_C)
    h_d = d // (2 * DH_D)
    p_b = rwkv_mu.shape[1]
    n_bp, n_bs = x_prompt.shape[0], x_sample.shape[0]
    past_len = cache_fox_k.shape[2]

    n_rows = _round_up(n_bp + n_bs, 16)
    c_all = jnp.pad(jnp.concatenate([c_prompt, c_sample], axis=0), ((0, n_rows - n_bp - n_bs), (0, 0)))
    mod4 = ada_modulation(c_all, ada_w, ada_b).reshape(depth, n_rows, 1, 6 * d)

    w_in_e = [_pad_cols(w_in_even[j], 2 * d_a + 3 * d_b + TAIL_BLOCK).astype(BF16) for j in range(w_in_even.shape[0])]
    w_in_o = [_pad_cols(w_in_odd[j], _round_up(w_in_odd.shape[2], 512)).astype(BF16) for j in range(w_in_odd.shape[0])]
    w_out_e = [w_out_even[j].astype(BF16) for j in range(w_out_even.shape[0])]
    w_out_o = [w_out_odd[j].astype(BF16) for j in range(w_out_odd.shape[0])]
    n_hp = peer_subkeys.shape[1]
    wq_t = [peer_wq[i].T.astype(BF16) for i in range(depth)]
    subk = [peer_subkeys[i].reshape(2 * n_hp, peer_subkeys.shape[3], peer_subkeys.shape[4]).astype(BF16)
            for i in range(depth)]
    u_all = peer_u.astype(BF16)
    vt_all = jnp.swapaxes(peer_v, 1, 2).astype(BF16)

    def run(x, row0, pos0, shift0, rwkv0, ret0, fox_past):
        B, T, _ = x.shape
        pos = pos0 + jnp.arange(T)
        v_rows, shifts, rwkvs, rets, ks, vs, lfs = [], [], [], [], [], [], []
        pending = None
        for i in range(depth):
            j = i // 2
            res = None if pending is None else (pending[0], pending[1], mod4, pending[3], row0, pending[2])
            x_new, h = resnorm(x, norm_g[i, 0], res=res, mod=(mod4, i, row0, 1, 0))
            x = x if x_new is None else x_new
            h2d = h.reshape(B * T, d)
            if i % 2 == 0:
                p = matmul(h2d, w_in_e[j]).reshape(B, T, -1)
                out_a, vn = spatial_gating(p, d_a, sgu_norm_g[j], sgu_w[j], sgu_b[j])
                r, w, k, v, a, g = rwkv_prep(p, d_b, 2 * d_a, shift0[j], rwkv_mu[j], rwkv_w0[j], rwkv_w_up[j],
                                             rwkv_a0[j], rwkv_a_up[j], rwkv_g_up[j])
                yield "relayout"
                yb, S = rwkv_scan(r, w, k, v, a, rwkv0[j], rwkv_k_k[j], rwkv_k_a[j], rwkv_r_k[j], rwkv_ln_w[j],
                                  rwkv_ln_b[j])
                mix = gate_concat(out_a, yb, g)
                y = matmul(mix.reshape(B * T, d), w_out_e[j]).reshape(B, T, d)
                v_rows.append(vn)
                shifts.append(p[:, -1:, 2 * d_a:2 * d_a + p_b])
                rwkvs.append(S)
            else:
                p = matmul(h2d, w_in_o[j]).reshape(B, T, -1)
                c_dq = 2 * h_c * DK_C + 2 * h_c * DV_C
                c_dk, c_dv, c_df = c_dq + h_d * DH_D, c_dq + 2 * h_d * DH_D, c_dq + 3 * h_d * DH_D
                out_c, R = retention(p, h_c, pos, ret0[j], ret_gn_g[j])
                logf = jax.nn.log_sigmoid(p[..., c_df:c_df + h_d] + fox_b_f[j])
                kd = p[..., c_dk:c_dk + h_d * DH_D]
                vd = p[..., c_dv:c_dv + h_d * DH_D]
                if fox_past is None:
                    F_all = jnp.cumsum(logf, axis=1)
                    od = fox_attention(p, c_dq // DH_D, p, c_dk // DH_D, p, c_dv // DH_D, F_all, h_d, T)
                else:
                    k_past, v_past, lf_past = fox_past[j]
                    k_all = jnp.concatenate([k_past.reshape(B, past_len, -1), kd], axis=1)
                    v_all = jnp.concatenate([v_past.reshape(B, past_len, -1), vd], axis=1)
                    F_all = jnp.cumsum(jnp.concatenate([lf_past, logf], axis=1), axis=1)
                    od = fox_attention(p, c_dq // DH_D, k_all, 0, v_all, 0, F_all, h_d, T)
                mix = jnp.concatenate([out_c, od], axis=-1)
                y = matmul(mix.reshape(B * T, d), w_out_o[j]).reshape(B, T, d)
                rets.append(R)
                ks.append(kd.reshape(B, T, h_d, DH_D))
                vs.append(vd.reshape(B, T, h_d, DH_D))
                lfs.append(logf)
            x, h = resnorm(x, norm_g[i, 1], res=(y, False, mod4, i, row0, 2), mod=(mod4, i, row0, 4, 3))
            y_t = peer_ffn(h, wq_t[i], subk[i], u_all, vt_all, i)
            if T % LANES == 0:
                pending = (y_t, True, 5, i)
            else:
                pending = (y_t.T.reshape(B, T, d), False, 5, i)
            yield "layer"
        _, out = resnorm(x, final_g, res=(pending[0], pending[1], mod4, pending[3], row0, pending[2]),
                         out_dtype=F32)
        return out, v_rows, shifts, rwkvs, rets, ks, vs, lfs

    n_even, n_odd = w_in_even.shape[0], w_in_odd.shape[0]
    run_p = run(
        x_prompt, 0, 0,
        [jnp.zeros((n_bp, 1, p_b), F32)] * n_even,
        [jnp.zeros((n_bp, h_b, DH_B, DH_B), F32)] * n_even,
        [jnp.zeros((n_bp, h_c, DK_C, DV_C), F32)] * n_odd,
        None)
    run_s = run(
        x_sample, n_bp, past_len,
        [state_rwkv_shift[j] for j in range(n_even)],
        [state_rwkv[j] for j in range(n_even)],
        [state_ret[j] for j in range(n_odd)],
        [(cache_fox_k[j], cache_fox_v[j], cache_fox_logf[j]) for j in range(n_odd)])

    def finish_layer(gen):
        while next(gen) != "layer":
            pass

    def result(gen):
        try:
            next(gen)
        except StopIteration as stop:
            return stop.value
        raise AssertionError("run() yielded after its last layer")

    for _ in range(depth):
        prompt_mid_layer = next(run_p) == "relayout"
        finish_layer(run_s)
        if prompt_mid_layer:
            finish_layer(run_p)
    y_p, _, sh_p, rw_p, rt_p, k_p, v_p, lf_p = result(run_p)
    y_s, vr_s, sh_s, rw_s, rt_s, k_s, v_s, lf_s = result(run_s)
    return (y_p, y_s, jnp.stack(vr_s), jnp.stack(sh_p), jnp.stack(sh_s), jnp.stack(rw_p), jnp.stack(rw_s),
            jnp.stack(rt_p), jnp.stack(rt_s), jnp.stack(k_p), jnp.stack(k_s), jnp.stack(v_p), jnp.stack(v_s),
            jnp.stack(lf_p), jnp.stack(lf_s))
```

```python
import functools
import math

import jax
import jax.numpy as jnp
from jax import lax
from jax.experimental import pallas as pl
from jax.experimental.pallas import tpu as pltpu

F32 = jnp.float32
BF16 = jnp.bfloat16

EPS = 1e-6
RWKV_GN_EPS = 64e-5
ROPE_BASE = 10000.0
RET_CHUNK = 64
MLP_CHUNK = 128
N_GROUPS_A = 8
DH_B = 64
DK_C = 128
DV_C = 256
DH_D = 128
TOPK_P = 16
LOG2E = 1.4426950408889634

LANES = 128
TAIL_BLOCK = 512
MIB = 1024 * 1024


def _round_up(n, m):
    return (n + m - 1) // m * m


def _params(sem, vmem_mib):
    return pltpu.CompilerParams(dimension_semantics=sem, vmem_limit_bytes=int(vmem_mib * MIB))


def _mm_kernel(a_ref, b_ref, o_ref):
    o_ref[...] = jnp.dot(a_ref[...], b_ref[...], preferred_element_type=F32).astype(o_ref.dtype)


def matmul(a, b, out_dtype=F32):
    M, K = a.shape
    N = b.shape[1]
    tm = min(M, 1024)
    tn = 512
    assert M % tm == 0 and N % tn == 0, (M, N)
    return pl.pallas_call(
        _mm_kernel,
        grid=(M // tm, N // tn),
        in_specs=[pl.BlockSpec((tm, K), lambda i, j: (i, 0)),
                  pl.BlockSpec((K, tn), lambda i, j: (0, j))],
        out_specs=pl.BlockSpec((tm, tn), lambda i, j: (i, j)),
        out_shape=jax.ShapeDtypeStruct((M, N), out_dtype),
        compiler_params=_params(("parallel", "parallel"), 48),
        name="matmul",
    )(a, b)


def _ada_kernel(c_ref, w_ref, b_ref, o_ref):
    c = c_ref[...]
    a = (c * jax.nn.sigmoid(c)).astype(BF16)
    o_ref[...] = jnp.dot(a, w_ref[...].astype(BF16), preferred_element_type=F32) + b_ref[...]


def ada_modulation(c_all, ada_w, ada_b):
    depth, d, n6 = ada_w.shape
    R = c_all.shape[0]
    tn = 512
    return pl.pallas_call(
        _ada_kernel,
        grid=(depth, n6 // tn),
        in_specs=[pl.BlockSpec((R, d), lambda i, j: (0, 0)),
                  pl.BlockSpec((None, d, tn), lambda i, j: (i, 0, j)),
                  pl.BlockSpec((None, 1, tn), lambda i, j: (i, 0, j))],
        out_specs=pl.BlockSpec((None, R, tn), lambda i, j: (i, 0, j)),
        out_shape=jax.ShapeDtypeStruct((depth, R, n6), F32),
        compiler_params=_params(("parallel", "parallel"), 40),
        name="ada_modulation",
    )(c_all, ada_w, ada_b.reshape(depth, 1, n6))


def _resnorm_kernel(*refs, has_res, y_transposed, has_mod):
    it = iter(refs)
    x_ref = next(it)
    if has_res:
        y_ref = next(it)
        gate_ref = next(it)
    g_ref = next(it)
    if has_mod:
        sc_ref = next(it)
        sh_ref = next(it)
    if has_res:
        xo_ref = next(it)
    h_ref = next(it)
    x = x_ref[...]
    if has_res:
        y = y_ref[...]
        if y_transposed:
            y = y.T
        x = x + gate_ref[...] * y
        xo_ref[...] = x
    xn = x * lax.rsqrt(jnp.mean(x * x, axis=-1, keepdims=True) + EPS) * g_ref[...]
    if has_mod:
        xn = xn * (1.0 + sc_ref[...]) + sh_ref[...]
    h_ref[...] = xn.astype(h_ref.dtype)


def resnorm(x, norm_g, *, res=None, mod=None, out_dtype=BF16):
    B, T, d = x.shape
    tt = min(T, 256)
    nt = T // tt
    row = pl.BlockSpec((None, tt, d), lambda b, i: (b, i, 0))
    args, specs = [x], [row]
    if res is not None:
        y, y_transposed, mod4, layer, row0, gk = res
        args.append(y)
        if y_transposed:
            specs.append(pl.BlockSpec((d, tt), lambda b, i: (0, b * nt + i)))
        else:
            specs.append(row)
        args.append(mod4)
        specs.append(pl.BlockSpec((None, None, 1, d), lambda b, i: (layer, row0 + b, 0, gk)))
    else:
        y_transposed = False
    args.append(norm_g.reshape(1, d))
    specs.append(pl.BlockSpec((1, d), lambda b, i: (0, 0)))
    if mod is not None:
        mod4m, layer_m, row0_m, sck, shk = mod
        args += [mod4m, mod4m]
        specs.append(pl.BlockSpec((None, None, 1, d), lambda b, i: (layer_m, row0_m + b, 0, sck)))
        specs.append(pl.BlockSpec((None, None, 1, d), lambda b, i: (layer_m, row0_m + b, 0, shk)))
    out_shape, out_specs = [], []
    if res is not None:
        out_shape.append(jax.ShapeDtypeStruct((B, T, d), F32))
        out_specs.append(row)
    out_shape.append(jax.ShapeDtypeStruct((B, T, d), out_dtype))
    out_specs.append(row)
    outs = pl.pallas_call(
        functools.partial(_resnorm_kernel, has_res=res is not None, y_transposed=y_transposed,
                          has_mod=mod is not None),
        grid=(B, nt),
        in_specs=specs,
        out_specs=out_specs,
        out_shape=out_shape,
        compiler_params=_params(("parallel", "parallel"), 48),
        name="resnorm",
    )(*args)
    if res is not None:
        return outs[0], outs[1]
    return None, outs[0]


def _sgu_kernel(pu_ref, pv_ref, ng_ref, w_ref, b_ref, oa_ref, vn_ref, *, groups):
    u = jax.nn.gelu(pu_ref[...])
    v = jax.nn.gelu(pv_ref[...])
    dg = v.shape[1] // groups
    for g in range(groups):
        sl = slice(g * dg, (g + 1) * dg)
        vg = v[:, sl]
        mean = jnp.mean(vg, axis=-1, keepdims=True)
        cen = vg - mean
        var = jnp.mean(cen * cen, axis=-1, keepdims=True)
        vn = cen * lax.rsqrt(var + EPS) * ng_ref[:, sl]
        vn_ref[:, sl] = vn
        s = jnp.dot(w_ref[g], vn.astype(BF16), preferred_element_type=F32) + b_ref[:, sl]
        oa_ref[:, sl] = (u[:, sl] * s).astype(oa_ref.dtype)


def spatial_gating(p, d_a, norm_g, w_s, b_s):
    B, T, _ = p.shape
    L = min(T, MLP_CHUNK)
    G = w_s.shape[0]
    dg = d_a // G
    w_tril = jnp.tril(w_s)[:, :L, :L].astype(BF16)
    b_full = jnp.repeat(b_s.T[:L], dg, axis=1)
    blk = lambda c: pl.BlockSpec((None, L, d_a), lambda b, i: (b, i, c))
    return pl.pallas_call(
        functools.partial(_sgu_kernel, groups=G),
        grid=(B, T // L),
        in_specs=[blk(0), blk(1),
                  pl.BlockSpec((1, d_a), lambda b, i: (0, 0)),
                  pl.BlockSpec((G, L, L), lambda b, i: (0, 0, 0)),
                  pl.BlockSpec((L, d_a), lambda b, i: (0, 0))],
        out_specs=[blk(0), blk(0)],
        out_shape=[jax.ShapeDtypeStruct((B, T, d_a), BF16), jax.ShapeDtypeStruct((B, T, d_a), F32)],
        compiler_params=_params(("parallel", "parallel"), 32),
        name="spatial_gating",
    )(p, p, norm_g.reshape(1, d_a), w_tril, b_full)


def _shifted(x_ref, halo_ref, state_ref, mu_ref):
    x = x_ref[...]
    first = jnp.where(pl.program_id(1) == 0, state_ref[...], halo_ref[7:8, :])
    rows = lax.broadcasted_iota(jnp.int32, x.shape, 0)
    prev = jnp.where(rows == 0, first, pltpu.roll(x, 1, axis=0))
    return x + (prev - x) * mu_ref[...]


def _rwkv_prep_kernel(r_ref, k_ref, v_ref, t_ref, rh_ref, kh_ref, vh_ref, th_ref,
                      sr_ref, sk_ref, sv_ref, st_ref, mr_ref, mk_ref, mv_ref, mt_ref,
                      w0_ref, wup_ref, a0_ref, aup_ref, gup_ref,
                      ro_ref, wo_ref, ko_ref, vo_ref, ao_ref, go_ref, *, lw, la, lg):
    r = _shifted(r_ref, rh_ref, sr_ref, mr_ref)
    k = _shifted(k_ref, kh_ref, sk_ref, mk_ref)
    v = _shifted(v_ref, vh_ref, sv_ref, mv_ref)
    t = _shifted(t_ref, th_ref, st_ref, mt_ref)
    wd, ad, gd = t[:, :lw], t[:, lw:lw + la], t[:, lw + la:lw + la + lg]
    wl = w0_ref[...] + jnp.dot(jnp.tanh(wd).astype(BF16), wup_ref[...], preferred_element_type=F32)
    w_log = -jax.nn.softplus(-wl) - 0.5
    wo_ref[...] = jnp.exp(-jnp.exp(w_log))
    a = jax.nn.sigmoid(a0_ref[...] + jnp.dot(ad.astype(BF16), aup_ref[...], preferred_element_type=F32))
    go_ref[...] = jnp.dot(jax.nn.sigmoid(gd).astype(BF16), gup_ref[...], preferred_element_type=F32)
    ro_ref[...] = r
    vo_ref[...] = v
    ao_ref[...] = a
    ko_ref[...] = k


def rwkv_prep(p, d_b, col0, shift_state, mu, w0, w_up, a0, a_up, g_up):
    B, T, _ = p.shape
    lw, la, lg = w_up.shape[0], a_up.shape[0], g_up.shape[0]
    tt = min(T, 128)
    cb = col0 // d_b
    tb = (col0 + 3 * d_b) // TAIL_BLOCK
    assert col0 % d_b == 0 and (col0 + 3 * d_b) % TAIL_BLOCK == 0 and lw + la + lg <= TAIL_BLOCK
    pad = TAIL_BLOCK - (lw + la + lg)

    def seg(c, width):
        return pl.BlockSpec((None, tt, width), lambda b, i: (b, i, c))

    def halo(c, width):
        return pl.BlockSpec((None, 8, width), lambda b, i: (b, jnp.maximum(i * (tt // 8) - 1, 0), c))

    def per_b(width):
        return pl.BlockSpec((None, 1, width), lambda b, i: (b, 0, 0))

    def full(shape):
        return pl.BlockSpec(shape, lambda b, i: tuple(0 for _ in shape))

    st = shift_state
    s_parts = [st[..., :d_b], st[..., d_b:2 * d_b], st[..., 2 * d_b:3 * d_b],
               jnp.pad(st[..., 3 * d_b:], ((0, 0), (0, 0), (0, pad)))]
    m_parts = [mu[None, :d_b], mu[None, d_b:2 * d_b], mu[None, 2 * d_b:3 * d_b],
               jnp.pad(mu[None, 3 * d_b:], ((0, 0), (0, pad)))]
    out_row = pl.BlockSpec((None, tt, d_b), lambda b, i: (b, i, 0))
    outs = pl.pallas_call(
        functools.partial(_rwkv_prep_kernel, lw=lw, la=la, lg=lg),
        grid=(B, T // tt),
        in_specs=[seg(cb, d_b), seg(cb + 1, d_b), seg(cb + 2, d_b), seg(tb, TAIL_BLOCK),
                  halo(cb, d_b), halo(cb + 1, d_b), halo(cb + 2, d_b), halo(tb, TAIL_BLOCK),
                  per_b(d_b), per_b(d_b), per_b(d_b), per_b(TAIL_BLOCK),
                  full((1, d_b)), full((1, d_b)), full((1, d_b)), full((1, TAIL_BLOCK)),
                  full((1, d_b)), full((lw, d_b)), full((1, d_b)), full((la, d_b)), full((lg, d_b))],
        out_specs=[out_row] * 6,
        out_shape=[jax.ShapeDtypeStruct((B, T, d_b), F32)] * 6,
        compiler_params=_params(("parallel", "arbitrary"), 56),
        name="rwkv_prep",
    )(p, p, p, p, p, p, p, p, *s_parts, *m_parts,
      w0[None], w_up.astype(BF16), a0[None], a_up.astype(BF16), g_up.astype(BF16))
    return outs


def _rwkv_scan_kernel(r_ref, w_ref, k_ref, v_ref, a_ref, kk_ref, ka_ref, rk_ref, lnw_ref, lnb_ref, s0_ref,
                      y_ref, so_ref, s_scr, av_scr, bv_scr, k2_scr, *, tc, dh):
    @pl.when(pl.program_id(1) == 0)
    def _():
        s_scr[...] = s0_ref[...]

    def key_vectors(t):
        kr = k_ref[t] * kk_ref[...]
        kk = kr * lax.rsqrt(jnp.maximum(jnp.sum(kr * kr, axis=0, keepdims=True), 1e-24))
        return -kk, kk * a_ref[t]

    def step(t, sa):
        a = a_ref[t]
        k2 = k_ref[t] * (1.0 + (a - 1.0) * ka_ref[...])
        k2_scr[...] = k2
        av_next, bv_next = key_vectors(jnp.minimum(t + 1, tc - 1))
        av_scr[...] = av_next
        v = v_ref[t]
        y = jnp.zeros_like(v)
        sa_next = jnp.zeros_like(v)
        for k in range(dh):
            s_new = (s_scr[k] * w_ref[t, k:k + 1, :] + sa * bv_scr[k:k + 1, :] + v * k2_scr[k:k + 1, :])
            s_scr[k] = s_new
            y = y + s_new * r_ref[t, k:k + 1, :]
            sa_next = sa_next + s_new * av_scr[k:k + 1, :]
        bv_scr[...] = bv_next
        mean = jnp.mean(y, axis=0, keepdims=True)
        cen = y - mean
        var = jnp.mean(cen * cen, axis=0, keepdims=True)
        yn = cen * lax.rsqrt(var + RWKV_GN_EPS) * lnw_ref[...] + lnb_ref[...]
        bonus = jnp.sum(r_ref[t] * k2 * rk_ref[...], axis=0, keepdims=True) * v
        y_ref[t] = yn + bonus
        return sa_next

    av0, bv0 = key_vectors(0)
    av_scr[...] = av0
    bv_scr[...] = bv0
    sa0 = jnp.zeros((dh, LANES), F32)
    for k in range(dh):
        sa0 = sa0 + s_scr[k] * av_scr[k:k + 1, :]
    lax.fori_loop(0, tc, step, sa0)

    @pl.when(pl.program_id(1) == pl.num_programs(1) - 1)
    def _():
        so_ref[...] = s_scr[...]


def rwkv_scan(r, w, k, v, a, S0, k_k, k_a, r_k, ln_w, ln_b):
    B, T, d_b = r.shape
    H = d_b // DH_B
    BH = B * H
    BHp = _round_up(BH, LANES)

    def to_lanes(x):
        x = x.reshape(B, T, H, DH_B).transpose(1, 3, 0, 2).reshape(T, DH_B, BH)
        return jnp.pad(x, ((0, 0), (0, 0), (0, BHp - BH)))

    def param_lanes(x):
        x = jnp.tile(x.T, (1, B))
        return jnp.pad(x, ((0, 0), (0, BHp - BH)))

    s0 = S0.transpose(3, 2, 0, 1).reshape(DH_B, DH_B, BH)
    s0 = jnp.pad(s0, ((0, 0), (0, 0), (0, BHp - BH)))
    tc = min(T, 64)
    seq = pl.BlockSpec((tc, DH_B, LANES), lambda l, i: (i, 0, l))
    par = pl.BlockSpec((DH_B, LANES), lambda l, i: (0, l))
    st = pl.BlockSpec((DH_B, DH_B, LANES), lambda l, i: (0, 0, l))
    y, s_fin = pl.pallas_call(
        functools.partial(_rwkv_scan_kernel, tc=tc, dh=DH_B),
        grid=(BHp // LANES, T // tc),
        in_specs=[seq] * 5 + [par] * 5 + [st],
        out_specs=[seq, st],
        out_shape=[jax.ShapeDtypeStruct((T, DH_B, BHp), F32), jax.ShapeDtypeStruct((DH_B, DH_B, BHp), F32)],
        scratch_shapes=[pltpu.VMEM((DH_B, DH_B, LANES), F32)] + [pltpu.VMEM((DH_B, LANES), F32)] * 3,
        compiler_params=_params(("parallel", "arbitrary"), 40),
        name="rwkv_scan",
    )(to_lanes(r), to_lanes(w), to_lanes(k), to_lanes(v), to_lanes(a),
      param_lanes(k_k.reshape(H, DH_B)), param_lanes(k_a.reshape(H, DH_B)), param_lanes(r_k),
      param_lanes(ln_w.reshape(H, DH_B)), param_lanes(ln_b.reshape(H, DH_B)), s0)
    y = y[:, :, :BH].reshape(T, DH_B, B, H).transpose(2, 0, 3, 1).reshape(B, T, d_b)
    s_fin = s_fin[:, :, :BH].reshape(DH_B, DH_B, B, H).transpose(2, 3, 1, 0)
    return y, s_fin


def _gate_concat_kernel(oa_ref, yb_ref, g_ref, o_ref, *, d_a):
    o_ref[:, :d_a] = oa_ref[...]
    o_ref[:, d_a:] = (yb_ref[...] * g_ref[...]).astype(o_ref.dtype)


def gate_concat(out_a, yb, g):
    B, T, d_a = out_a.shape
    d_b = yb.shape[-1]
    tt = min(T, 256)
    blk = lambda w: pl.BlockSpec((None, tt, w), lambda b, i: (b, i, 0))
    return pl.pallas_call(
        functools.partial(_gate_concat_kernel, d_a=d_a),
        grid=(B, T // tt),
        in_specs=[blk(d_a), blk(d_b), blk(d_b)],
        out_specs=blk(d_a + d_b),
        out_shape=jax.ShapeDtypeStruct((B, T, d_a + d_b), BF16),
        compiler_params=_params(("parallel", "parallel"), 32),
        name="gate_concat",
    )(out_a, yb, g)


def _rope(x, cos2, sin2):
    return x * cos2 + pltpu.roll(x, x.shape[-1] // 2, axis=1) * sin2


def _retention_kernel(q_ref, k_ref, v_ref, g_ref, cos_ref, sin_ref, dm_ref, kd_ref, qd_ref, gl_ref, gn_ref,
                      r0_ref, o_ref, ro_ref, r_scr, *, L, n_chunks):
    @pl.when(pl.program_id(2) == 0)
    def _():
        r_scr[...] = r0_ref[...]

    scale = DK_C ** -0.5
    for c in range(n_chunks):
        rows = slice(c * L, (c + 1) * L)
        cos2, sin2 = cos_ref[rows, :], sin_ref[rows, :]
        q = _rope(q_ref[rows, :], cos2, sin2)
        k = _rope(k_ref[rows, :], cos2, sin2) * scale
        v = v_ref[rows, :].astype(BF16)
        att = lax.dot_general(q.astype(BF16), k.astype(BF16), (((1,), (1,)), ((), ())),
                              preferred_element_type=F32) * dm_ref[...]
        R = r_scr[...]
        o = jnp.dot(att.astype(BF16), v, preferred_element_type=F32)
        o = o + jnp.dot((q * qd_ref[...]).astype(BF16), R.astype(BF16), preferred_element_type=F32)
        kd_t = (k * kd_ref[...]).T.astype(BF16)
        r_scr[...] = R * gl_ref[...] + jnp.dot(kd_t, v, preferred_element_type=F32)
        mean = jnp.mean(o, axis=-1, keepdims=True)
        cen = o - mean
        var = jnp.mean(cen * cen, axis=-1, keepdims=True)
        on = cen * lax.rsqrt(var + EPS) * gn_ref[...]
        gate = g_ref[rows, :]
        o_ref[rows, :] = (gate * jax.nn.sigmoid(gate) * on).astype(o_ref.dtype)

    @pl.when(pl.program_id(2) == pl.num_programs(2) - 1)
    def _():
        ro_ref[...] = r_scr[...]


def retention(p, h_c, pos, R0, gn_g):
    B, T, _ = p.shape
    L = math.gcd(T, RET_CHUNK)
    tr = min(T, 512)
    n_chunks = tr // L
    half = DK_C // 2
    inv = ROPE_BASE ** (-jnp.arange(half, dtype=F32) / half)
    ang = pos.astype(F32)[:, None] * inv[None, :]
    cos2 = jnp.concatenate([jnp.cos(ang), jnp.cos(ang)], axis=-1)
    sin2 = jnp.concatenate([-jnp.sin(ang), jnp.sin(ang)], axis=-1)
    log_g = jnp.log1p(-jnp.exp2(-5.0 - jnp.arange(h_c, dtype=F32)))
    idx = jnp.arange(L, dtype=F32)
    rel = idx[:, None] - idx[None, :]
    dmat = jnp.where(rel >= 0, jnp.exp(jnp.maximum(rel, 0.0) * log_g[:, None, None]), 0.0)
    k_dec = jnp.exp((L - 1 - idx)[None, :, None] * log_g[:, None, None])
    q_dec = jnp.exp((idx + 1)[None, :, None] * log_g[:, None, None])
    k_dec = jnp.broadcast_to(k_dec, (h_c, L, DK_C))
    q_dec = jnp.broadcast_to(q_dec, (h_c, L, DK_C))
    g_l = jnp.broadcast_to(jnp.exp(L * log_g)[:, None, None], (h_c, 1, DV_C))

    def col(width, c0):
        return pl.BlockSpec((None, tr, width), lambda b, h, i: (b, i, c0 + h))

    def head(shape):
        return pl.BlockSpec((None,) + shape, lambda b, h, i: (h,) + tuple(0 for _ in shape))

    trig = pl.BlockSpec((tr, DK_C), lambda b, h, i: (i, 0))
    state = pl.BlockSpec((None, None, DK_C, DV_C), lambda b, h, i: (b, h, 0, 0))
    return pl.pallas_call(
        functools.partial(_retention_kernel, L=L, n_chunks=n_chunks),
        grid=(B, h_c, T // tr),
        in_specs=[col(DK_C, 0), col(DK_C, h_c), col(DV_C, h_c), col(DV_C, 2 * h_c), trig, trig,
                  head((L, L)), head((L, DK_C)), head((L, DK_C)), head((1, DV_C)),
                  pl.BlockSpec((1, DV_C), lambda b, h, i: (0, h)), state],
        out_specs=[pl.BlockSpec((None, tr, DV_C), lambda b, h, i: (b, i, h)), state],
        out_shape=[jax.ShapeDtypeStruct((B, T, h_c * DV_C), BF16),
                   jax.ShapeDtypeStruct((B, h_c, DK_C, DV_C), F32)],
        scratch_shapes=[pltpu.VMEM((DK_C, DV_C), F32)],
        compiler_params=_params(("parallel", "parallel", "arbitrary"), 32),
        name="retention",
    )(p, p, p, p, cos2, sin2, dmat, k_dec, q_dec, g_l, gn_g.reshape(1, h_c * DV_C), R0)


def _fox_kernel(q_ref, k_ref, v_ref, fq_ref, fk_ref, o_ref, kb_scr, vt_scr, acc_scr, *, tq, tkc, past):
    qi = pl.program_id(2)
    n_kv = k_ref.shape[0] // tkc

    @pl.when(qi == 0)
    def _():
        for c in range(n_kv):
            rows = slice(c * tkc, (c + 1) * tkc)
            kb_scr[rows, :] = k_ref[rows, :].astype(BF16)
            vt_scr[:, rows] = v_ref[rows, :].T.astype(BF16)

    q = (q_ref[...] * (DH_D ** -0.5 * LOG2E)).astype(BF16)
    fq = fq_ref[...]
    n_rep = max(tq // LANES, 1)

    def scores(c, masked):
        r0 = 0 if n_kv == 1 else pl.multiple_of(c * tkc, tkc)
        s = lax.dot_general(kb_scr[pl.ds(r0, tkc), :], q, (((1,), (1,)), ((), ())), preferred_element_type=F32)
        fk = fk_ref[pl.ds(r0, tkc), :]
        fk = jnp.concatenate([fk] * n_rep, axis=1) if tq >= LANES else fk[:, :tq]
        s = s + (fq - fk)
        if masked:
            kpos = r0 + lax.broadcasted_iota(jnp.int32, s.shape, 0)
            qpos = past + qi * tq + lax.broadcasted_iota(jnp.int32, s.shape, 1)
            s = jnp.where(kpos <= qpos, s, -jnp.inf)
        return r0, s

    def chunks(cs, carry, masked):
        m_old, l_old = carry
        parts = [scores(c, masked) for c in cs]
        m_new = m_old
        for _, s in parts:
            m_new = jnp.maximum(m_new, jnp.max(s, axis=0, keepdims=True))
        alpha = jnp.exp2(m_old - m_new)
        l_new = alpha * l_old
        upd = alpha * acc_scr[...]
        for r0, s in parts:
            p = jnp.exp2(s - m_new)
            l_new = l_new + jnp.sum(p, axis=0, keepdims=True)
            upd = upd + jnp.dot(vt_scr[:, pl.ds(r0, tkc)], p.astype(BF16), preferred_element_type=F32)
        acc_scr[...] = upd
        return m_new, l_new

    def sweep(lo, hi, carry, masked):
        for width in (4, 2, 1):
            n_grp = (hi - lo) // width
            carry = lax.fori_loop(
                0, n_grp, lambda i, cr, lo=lo, width=width: chunks(tuple(lo + width * i + u for u in range(width)),
                                                                    cr, masked), carry)
            lo = lo + width * n_grp
        return carry

    acc_scr[...] = jnp.zeros_like(acc_scr)
    carry = (jnp.full((1, tq), -jnp.inf, F32), jnp.zeros((1, tq), F32))
    n_full = (past + qi * tq + 1) // tkc
    n_need = (past + qi * tq + tq - 1) // tkc + 1
    carry = sweep(0, n_full, carry, False)
    m_fin, l_fin = sweep(n_full, n_need, carry, True)
    o_ref[...] = (acc_scr[...] / l_fin).T.astype(o_ref.dtype)


def fox_attention(q_src, q_col, k_src, k_col, v_src, v_col, F_all, n_heads, T):
    B = q_src.shape[0]
    Tk = k_src.shape[1]
    past = Tk - T
    tq = min(T, 512)
    tkc = 256 if Tk % 256 == 0 else Tk
    assert T % tq == 0
    f_heads = F_all.transpose(0, 2, 1) * LOG2E
    f_row = f_heads[:, :, None, past:]
    f_rep = jnp.broadcast_to(f_heads[..., None], (B, n_heads, Tk, LANES))

    def kv(col0):
        return pl.BlockSpec((None, Tk, DH_D), lambda b, h, i: (b, 0, col0 + h))

    return pl.pallas_call(
        functools.partial(_fox_kernel, tq=tq, tkc=tkc, past=past),
        grid=(B, n_heads, T // tq),
        in_specs=[pl.BlockSpec((None, tq, DH_D), lambda b, h, i: (b, i, q_col + h)),
                  kv(k_col), kv(v_col),
                  pl.BlockSpec((None, None, 1, tq), lambda b, h, i: (b, h, 0, i)),
                  pl.BlockSpec((None, None, Tk, LANES), lambda b, h, i: (b, h, 0, 0))],
        out_specs=pl.BlockSpec((None, tq, DH_D), lambda b, h, i: (b, i, h)),
        out_shape=jax.ShapeDtypeStruct((B, T, n_heads * DH_D), BF16),
        scratch_shapes=[pltpu.VMEM((Tk, DH_D), BF16), pltpu.VMEM((DH_D, Tk), BF16), pltpu.VMEM((DH_D, tq), F32)],
        compiler_params=_params(("parallel", "parallel", "arbitrary"), 40),
        name="fox_attention",
    )(q_src, k_src, v_src, f_row, f_rep)


SUBLANES = 8


def _bitonic_merge_desc(v):
    v = list(v)
    n = len(v)
    step = n // 2
    while step >= 1:
        for i in range(n):
            if i & step == 0:
                a, b = v[i], v[i + step]
                v[i], v[i + step] = jnp.maximum(a, b), jnp.minimum(a, b)
        step //= 2
    return v


def _bitonic_sort_desc(v):
    v = list(v)
    n = len(v)
    size = 2
    while size <= n:
        step = size // 2
        while step >= 1:
            for i in range(n):
                if i & step == 0:
                    a, b = v[i], v[i + step]
                    hi, lo = jnp.maximum(a, b), jnp.minimum(a, b)
                    v[i], v[i + step] = (hi, lo) if i & size == 0 else (lo, hi)
            step //= 2
        size *= 2
    return v


def _merge_top(a, b):
    n = len(a)
    c = [jnp.maximum(a[i], b[n - 1 - i]) if n - 1 - i < len(b) else a[i] for i in range(n)]
    return _bitonic_merge_desc(c)


def _top_of_keys(s):
    slots = [s[SUBLANES * j:SUBLANES * (j + 1), :] for j in range(s.shape[0] // SUBLANES)]
    assert len(slots) == TOPK_P
    v = _bitonic_sort_desc(slots)
    shift = SUBLANES // 2
    while shift >= 1:
        partner = [pltpu.roll(x, shift, axis=0) for x in v]
        v = _merge_top(v, partner)
        shift //= 2
    return v


def _peer_score_kernel(h_ref, wq_ref, sk_ref, s1_ref, s2_ref, tau_ref, mz_ref, *, n_heads, n_keys):
    qT = lax.dot_general(wq_ref[...], h_ref[...], (((1,), (1,)), ((), ())), preferred_element_type=F32)
    dq = qT.shape[0] // (2 * n_heads)
    t = qT.shape[1]
    head_row = lax.broadcasted_iota(jnp.int32, (SUBLANES, t), 0)
    tops = [[jnp.zeros((SUBLANES, t), F32)] * TOPK_P for _ in range(2)]
    for hd in range(n_heads):
        for c in range(2):
            hc = 2 * hd + c
            q = qT[hc * dq:(hc + 1) * dq, :].astype(BF16)
            s = jnp.dot(sk_ref[hc], q, preferred_element_type=F32)
            if c == 0:
                s1_ref[:, hd, :] = s
            else:
                s2_ref[hd] = s
            top = _top_of_keys(s)
            tops[c] = [jnp.where(head_row == hd, top[i], tops[c][i]) for i in range(TOPK_P)]
    v1, v2 = tops
    best = [v1[0] + v2[b] for b in range(TOPK_P)]
    for a in range(1, TOPK_P):
        best = _merge_top(best, [v1[a] + v2[b] for b in range(TOPK_P // (a + 1))])
    z = jnp.zeros_like(best[0])
    for s_k in best:
        z = z + jnp.exp(s_k - best[0])
    tau_ref[...] = best[TOPK_P - 1]
    mz_ref[...] = best[0] + jnp.log(z)


def peer_scores(h2d, wq_t, subkeys_b):
    N, d = h2d.shape
    n_hc, n_keys, dq = subkeys_b.shape
    n_heads = n_hc // 2
    assert n_heads == SUBLANES and n_keys == TOPK_P * SUBLANES
    tt = min(N, 256)
    return pl.pallas_call(
        functools.partial(_peer_score_kernel, n_heads=n_heads, n_keys=n_keys),
        grid=(N // tt,),
        in_specs=[pl.BlockSpec((tt, d), lambda i: (i, 0)),
                  pl.BlockSpec(wq_t.shape, lambda i: (0, 0), pipeline_mode=pl.Buffered(1)),
                  pl.BlockSpec(subkeys_b.shape, lambda i: (0, 0, 0))],
        out_specs=[pl.BlockSpec((n_keys, n_heads, tt), lambda i: (0, 0, i)),
                   pl.BlockSpec((n_heads, n_keys, tt), lambda i: (0, 0, i)),
                   pl.BlockSpec((n_heads, tt), lambda i: (0, i)),
                   pl.BlockSpec((n_heads, tt), lambda i: (0, i))],
        out_shape=[jax.ShapeDtypeStruct((n_keys, n_heads, N), F32),
                   jax.ShapeDtypeStruct((n_heads, n_keys, N), F32),
                   jax.ShapeDtypeStruct((n_heads, N), F32),
                   jax.ShapeDtypeStruct((n_heads, N), F32)],
        compiler_params=_params(("parallel",), 48),
        name="peer_scores",
    )(h2d, wq_t, subkeys_b)


def _peer_expert_kernel(h_ref, u_ref, vt_ref, s1_ref, s2_ref, tau_ref, mz_ref, o_ref, w0_scr, w1_scr, g0_scr, g1_scr,
                        *, n_heads, n_keys, rows, n_tiles):
    j = pl.program_id(1)
    tt = h_ref.shape[0]
    d = o_ref.shape[0]
    cw = min(tt, LANES)
    w_scr = (w0_scr, w1_scr)
    g_scr = (g0_scr, g1_scr)
    gate_pieces = [(r, c0) for r in range(rows) for c0 in range(0, tt, cw)]
    n_g = len(gate_pieces)

    def gate_piece(tile, g_ref, r, c0):
        cols = slice(c0, c0 + cw)
        s1_heads = s1_ref[tile * rows + r]
        g = jnp.zeros((n_keys, cw), F32)
        for hd in range(n_heads):
            s = s1_heads[hd:hd + 1, cols] + s2_ref[hd, :, cols]
            e = jnp.exp(s - mz_ref[hd:hd + 1, cols])
            g = g + jnp.where(s >= tau_ref[hd:hd + 1, cols], e, 0.0)
        g_ref[r * n_keys:(r + 1) * n_keys, cols] = g

    def score_piece(g_ref, w_ref, piece, n_p):
        cols = slice(piece * (tt // n_p), (piece + 1) * (tt // n_p))
        hT = lax.dot_general(u_ref[...], h_ref[cols, :], (((1,), (1,)), ((), ())), preferred_element_type=F32)
        w_ref[:, cols] = (jax.nn.gelu(hT) * g_ref[:, cols]).astype(BF16)

    def accumulate_piece(w_ref, m, n_m):
        rws = slice(m * (d // n_m), (m + 1) * (d // n_m))
        o_ref[rws, :] += jnp.dot(vt_ref[rws, :], w_ref[...], preferred_element_type=F32)

    n_sp = 2 if tt >= 2 * LANES else 1

    @pl.when(j == 0)
    def _():
        o_ref[...] = jnp.zeros_like(o_ref)
        for r, c0 in gate_pieces:
            gate_piece(j, g_scr[0], r, c0)
        for p in range(n_sp):
            score_piece(g_scr[0], w_scr[0], p, n_sp)
        for r, c0 in gate_pieces:
            gate_piece(j + 1, g_scr[1], r, c0)

    for par in range(2):
        @pl.when(jnp.logical_and(jnp.logical_and(j > 0, j < n_tiles), j % 2 == par))
        def _():
            nxt = jnp.minimum(j + 1, n_tiles - 1)
            for m in range(n_g):
                accumulate_piece(w_scr[1 - par], m, n_g)
                gate_piece(nxt, g_scr[1 - par], *gate_pieces[m])
            for p in range(n_sp):
                score_piece(g_scr[par], w_scr[par], p, n_sp)

    @pl.when(j == n_tiles)
    def _():
        accumulate_piece(w_scr[(n_tiles - 1) % 2], 0, 1)


def peer_experts(h2d, u_all, vt_all, layer, s1, s2, tau, mz):
    N, d = h2d.shape
    n_exp = u_all.shape[1]
    n_heads, n_keys, _ = s2.shape
    tt = min(N, 512)
    te = 512
    rows = te // n_keys
    n_tiles = n_exp // te
    assert n_tiles % 2 == 0
    once = pl.Buffered(1)
    return pl.pallas_call(
        functools.partial(_peer_expert_kernel, n_heads=n_heads, n_keys=n_keys, rows=rows, n_tiles=n_tiles),
        grid=(N // tt, n_tiles + 1),
        in_specs=[pl.BlockSpec((tt, d), lambda i, j: (i, 0), pipeline_mode=once),
                  pl.BlockSpec((None, te, d), lambda i, j: (layer, jnp.minimum(j, n_tiles - 1), 0)),
                  pl.BlockSpec((None, d, te), lambda i, j: (layer, 0, jnp.maximum(j - 1, 0))),
                  pl.BlockSpec((n_keys, n_heads, tt), lambda i, j: (0, 0, i), pipeline_mode=once),
                  pl.BlockSpec((n_heads, n_keys, tt), lambda i, j: (0, 0, i), pipeline_mode=once),
                  pl.BlockSpec((n_heads, tt), lambda i, j: (0, i), pipeline_mode=once),
                  pl.BlockSpec((n_heads, tt), lambda i, j: (0, i), pipeline_mode=once)],
        out_specs=pl.BlockSpec((d, tt), lambda i, j: (0, i)),
        out_shape=jax.ShapeDtypeStruct((d, N), F32),
        scratch_shapes=[pltpu.VMEM((te, tt), BF16)] * 2 + [pltpu.VMEM((te, tt), F32)] * 2,
        compiler_params=_params(("parallel", "arbitrary"), 56),
        name="peer_experts",
    )(h2d, u_all, vt_all, s1, s2, tau, mz)


def peer_ffn(h, wq_t, subkeys_b, u_all, vt_all, layer):
    B, T, d = h.shape
    h2d = h.reshape(B * T, d)
    s1, s2, tau, mz = peer_scores(h2d, wq_t, subkeys_b)
    return peer_experts(h2d, u_all, vt_all, layer, s1, s2, tau, mz)


def _pad_cols(w, n):
    return jnp.pad(w, ((0, 0), (0, n - w.shape[1])))


def kernel(x_prompt, x_sample, c_prompt, c_sample, state_rwkv_shift, state_rwkv, state_ret, cache_fox_k, cache_fox_v, cache_fox_logf, ada_w, ada_b, norm_g, final_g, w_in_even, w_out_even, sgu_norm_g, sgu_w, sgu_b, rwkv_mu, rwkv_w0, rwkv_w_up, rwkv_a0, rwkv_a_up, rwkv_g_up, rwkv_k_k, rwkv_k_a, rwkv_r_k, rwkv_ln_w, rwkv_ln_b, w_in_odd, w_out_odd, ret_gn_g, fox_b_f, peer_wq, peer_subkeys, peer_u, peer_v):
    depth = ada_w.shape[0]
    d = x_prompt.shape[-1]
    d_a = d_b = d // 2
    h_b = d_b // DH_B
    h_c = d // (2 * DV_C)
    h_d = d // (2 * DH_D)
    p_b = rwkv_mu.shape[1]
    n_bp, n_bs = x_prompt.shape[0], x_sample.shape[0]
    past_len = cache_fox_k.shape[2]

    n_rows = _round_up(n_bp + n_bs, 16)
    c_all = jnp.pad(jnp.concatenate([c_prompt, c_sample], axis=0), ((0, n_rows - n_bp - n_bs), (0, 0)))
    mod4 = ada_modulation(c_all, ada_w, ada_b).reshape(depth, n_rows, 1, 6 * d)

    w_in_e = [_pad_cols(w_in_even[j], 2 * d_a + 3 * d_b + TAIL_BLOCK).astype(BF16) for j in range(w_in_even.shape[0])]
    w_in_o = [_pad_cols(w_in_odd[j], _round_up(w_in_odd.shape[2], 512)).astype(BF16) for j in range(w_in_odd.shape[0])]
    w_out_e = [w_out_even[j].astype(BF16) for j in range(w_out_even.shape[0])]
    w_out_o = [w_out_odd[j].astype(BF16) for j in range(w_out_odd.shape[0])]
    n_hp = peer_subkeys.shape[1]
    wq_t = [peer_wq[i].T.astype(BF16) for i in range(depth)]
    subk = [peer_subkeys[i].reshape(2 * n_hp, peer_subkeys.shape[3], peer_subkeys.shape[4]).astype(BF16)
            for i in range(depth)]
    u_all = peer_u.astype(BF16)
    vt_all = jnp.swapaxes(peer_v, 1, 2).astype(BF16)

    def run(x, row0, pos0, shift0, rwkv0, ret0, fox_past):
        B, T, _ = x.shape
        pos = pos0 + jnp.arange(T)
        v_rows, shifts, rwkvs, rets, ks, vs, lfs = [], [], [], [], [], [], []
        pending = None
        for i in range(depth):
            j = i // 2
            res = None if pending is None else (pending[0], pending[1], mod4, pending[3], row0, pending[2])
            x_new, h = resnorm(x, norm_g[i, 0], res=res, mod=(mod4, i, row0, 1, 0))
            x = x if x_new is None else x_new
            h2d = h.reshape(B * T, d)
            if i % 2 == 0:
                p = matmul(h2d, w_in_e[j]).reshape(B, T, -1)
                out_a, vn = spatial_gating(p, d_a, sgu_norm_g[j], sgu_w[j], sgu_b[j])
                r, w, k, v, a, g = rwkv_prep(p, d_b, 2 * d_a, shift0[j], rwkv_mu[j], rwkv_w0[j], rwkv_w_up[j],
                                             rwkv_a0[j], rwkv_a_up[j], rwkv_g_up[j])
                yield "relayout"
                yb, S = rwkv_scan(r, w, k, v, a, rwkv0[j], rwkv_k_k[j], rwkv_k_a[j], rwkv_r_k[j], rwkv_ln_w[j],
                                  rwkv_ln_b[j])
                mix = gate_concat(out_a, yb, g)
                y = matmul(mix.reshape(B * T, d), w_out_e[j]).reshape(B, T, d)
                v_rows.append(vn)
                shifts.append(p[:, -1:, 2 * d_a:2 * d_a + p_b])
                rwkvs.append(S)
            else:
                p = matmul(h2d, w_in_o[j]).reshape(B, T, -1)
                c_dq = 2 * h_c * DK_C + 2 * h_c * DV_C
                c_dk, c_dv, c_df = c_dq + h_d * DH_D, c_dq + 2 * h_d * DH_D, c_dq + 3 * h_d * DH_D
                out_c, R = retention(p, h_c, pos, ret0[j], ret_gn_g[j])
                logf = jax.nn.log_sigmoid(p[..., c_df:c_df + h_d] + fox_b_f[j])
                kd = p[..., c_dk:c_dk + h_d * DH_D]
                vd = p[..., c_dv:c_dv + h_d * DH_D]
                if fox_past is None:
                    F_all = jnp.cumsum(logf, axis=1)
                    od = fox_attention(p, c_dq // DH_D, p, c_dk // DH_D, p, c_dv // DH_D, F_all, h_d, T)
                else:
                    k_past, v_past, lf_past = fox_past[j]
                    k_all = jnp.concatenate([k_past.reshape(B, past_len, -1), kd], axis=1)
                    v_all = jnp.concatenate([v_past.reshape(B, past_len, -1), vd], axis=1)
                    F_all = jnp.cumsum(jnp.concatenate([lf_past, logf], axis=1), axis=1)
                    od = fox_attention(p, c_dq // DH_D, k_all, 0, v_all, 0, F_all, h_d, T)
                mix = jnp.concatenate([out_c, od], axis=-1)
                y = matmul(mix.reshape(B * T, d), w_out_o[j]).reshape(B, T, d)
                rets.append(R)
                ks.append(kd.reshape(B, T, h_d, DH_D))
                vs.append(vd.reshape(B, T, h_d, DH_D))
                lfs.append(logf)
            x, h = resnorm(x, norm_g[i, 1], res=(y, False, mod4, i, row0, 2), mod=(mod4, i, row0, 4, 3))
            y_t = peer_ffn(h, wq_t[i], subk[i], u_all, vt_all, i)
            if T % LANES == 0:
                pending = (y_t, True, 5, i)
            else:
                pending = (y_t.T.reshape(B, T, d), False, 5, i)
            yield "layer"
        _, out = resnorm(x, final_g, res=(pending[0], pending[1], mod4, pending[3], row0, pending[2]),
                         out_dtype=F32)
        return out, v_rows, shifts, rwkvs, rets, ks, vs, lfs

    n_even, n_odd = w_in_even.shape[0], w_in_odd.shape[0]
    run_p = run(
        x_prompt, 0, 0,
        [jnp.zeros((n_bp, 1, p_b), F32)] * n_even,
        [jnp.zeros((n_bp, h_b, DH_B, DH_B), F32)] * n_even,
        [jnp.zeros((n_bp, h_c, DK_C, DV_C), F32)] * n_odd,
        None)
    run_s = run(
        x_sample, n_bp, past_len,
        [state_rwkv_shift[j] for j in range(n_even)],
        [state_rwkv[j] for j in range(n_even)],
        [state_ret[j] for j in range(n_odd)],
        [(cache_fox_k[j], cache_fox_v[j], cache_fox_logf[j]) for j in range(n_odd)])

    def finish_layer(gen):
        while next(gen) != "layer":
            pass

    def result(gen):
        try:
            next(gen)
        except StopIteration as stop:
            return stop.value
        raise AssertionError("run() yielded after its last layer")

    for _ in range(depth):
        prompt_mid_layer = next(run_p) == "relayout"
        finish_layer(run_s)
        if prompt_mid_layer:
            finish_layer(run_p)
    y_p, _, sh_p, rw_p, rt_p, k_p, v_p, lf_p = result(run_p)
    y_s, vr_s, sh_s, rw_s, rt_s, k_s, v_s, lf_s = result(run_s)
    return (y_p, y_s, jnp.stack(vr_s), jnp.stack(sh_p), jnp.stack(sh_s), jnp.stack(rw_p), jnp.stack(rw_s),
            jnp.stack(rt_p), jnp.stack(rt_s), jnp.stack(k_p), jnp.stack(k_s), jnp.stack(v_p), jnp.stack(v_s),
            jnp.stack(lf_p), jnp.stack(lf_s))
```

```python
import functools
import math

import jax
import jax.numpy as jnp
from jax import lax
from jax.experimental import pallas as pl
from jax.experimental.pallas import tpu as pltpu

F32 = jnp.float32
BF16 = jnp.bfloat16

EPS = 1e-6
RWKV_GN_EPS = 64e-5
ROPE_BASE = 10000.0
RET_CHUNK = 64
MLP_CHUNK = 128
N_GROUPS_A = 8
DH_B = 64
DK_C = 128
DV_C = 256
DH_D = 128
TOPK_P = 16
LOG2E = 1.4426950408889634

LANES = 128
TAIL_BLOCK = 512
MIB = 1024 * 1024


def _round_up(n, m):
    return (n + m - 1) // m * m


def _params(sem, vmem_mib):
    return pltpu.CompilerParams(dimension_semantics=sem, vmem_limit_bytes=int(vmem_mib * MIB))


def _mm_kernel(a_ref, b_ref, o_ref):
    o_ref[...] = jnp.dot(a_ref[...], b_ref[...], preferred_element_type=F32).astype(o_ref.dtype)


def matmul(a, b, out_dtype=F32):
    M, K = a.shape
    N = b.shape[1]
    tm = min(M, 1024)
    tn = 512
    assert M % tm == 0 and N % tn == 0, (M, N)
    return pl.pallas_call(
        _mm_kernel,
        grid=(M // tm, N // tn),
        in_specs=[pl.BlockSpec((tm, K), lambda i, j: (i, 0)),
                  pl.BlockSpec((K, tn), lambda i, j: (0, j))],
        out_specs=pl.BlockSpec((tm, tn), lambda i, j: (i, j)),
        out_shape=jax.ShapeDtypeStruct((M, N), out_dtype),
        compiler_params=_params(("parallel", "parallel"), 48),
        name="matmul",
    )(a, b)


def _ada_kernel(c_ref, w_ref, b_ref, o_ref):
    c = c_ref[...]
    a = (c * jax.nn.sigmoid(c)).astype(BF16)
    o_ref[...] = jnp.dot(a, w_ref[...].astype(BF16), preferred_element_type=F32) + b_ref[...]


def ada_modulation(c_all, ada_w, ada_b):
    depth, d, n6 = ada_w.shape
    R = c_all.shape[0]
    tn = 512
    return pl.pallas_call(
        _ada_kernel,
        grid=(depth, n6 // tn),
        in_specs=[pl.BlockSpec((R, d), lambda i, j: (0, 0)),
                  pl.BlockSpec((None, d, tn), lambda i, j: (i, 0, j)),
                  pl.BlockSpec((None, 1, tn), lambda i, j: (i, 0, j))],
        out_specs=pl.BlockSpec((None, R, tn), lambda i, j: (i, 0, j)),
        out_shape=jax.ShapeDtypeStruct((depth, R, n6), F32),
        compiler_params=_params(("parallel", "parallel"), 40),
        name="ada_modulation",
    )(c_all, ada_w, ada_b.reshape(depth, 1, n6))


def _resnorm_kernel(*refs, has_res, y_transposed, has_mod):
    it = iter(refs)
    x_ref = next(it)
    if has_res:
        y_ref = next(it)
        gate_ref = next(it)
    g_ref = next(it)
    if has_mod:
        sc_ref = next(it)
        sh_ref = next(it)
    if has_res:
        xo_ref = next(it)
    h_ref = next(it)
    x = x_ref[...]
    if has_res:
        y = y_ref[...]
        if y_transposed:
            y = y.T
        x = x + gate_ref[...] * y
        xo_ref[...] = x
    xn = x * lax.rsqrt(jnp.mean(x * x, axis=-1, keepdims=True) + EPS) * g_ref[...]
    if has_mod:
        xn = xn * (1.0 + sc_ref[...]) + sh_ref[...]
    h_ref[...] = xn.astype(h_ref.dtype)


def resnorm(x, norm_g, *, res=None, mod=None, out_dtype=BF16):
    B, T, d = x.shape
    tt = min(T, 256)
    nt = T // tt
    row = pl.BlockSpec((None, tt, d), lambda b, i: (b, i, 0))
    args, specs = [x], [row]
    if res is not None:
        y, y_transposed, mod4, layer, row0, gk = res
        args.append(y)
        if y_transposed:
            specs.append(pl.BlockSpec((d, tt), lambda b, i: (0, b * nt + i)))
        else:
            specs.append(row)
        args.append(mod4)
        specs.append(pl.BlockSpec((None, None, 1, d), lambda b, i: (layer, row0 + b, 0, gk)))
    else:
        y_transposed = False
    args.append(norm_g.reshape(1, d))
    specs.append(pl.BlockSpec((1, d), lambda b, i: (0, 0)))
    if mod is not None:
        mod4m, layer_m, row0_m, sck, shk = mod
        args += [mod4m, mod4m]
        specs.append(pl.BlockSpec((None, None, 1, d), lambda b, i: (layer_m, row0_m + b, 0, sck)))
        specs.append(pl.BlockSpec((None, None, 1, d), lambda b, i: (layer_m, row0_m + b, 0, shk)))
    out_shape, out_specs = [], []
    if res is not None:
        out_shape.append(jax.ShapeDtypeStruct((B, T, d), F32))
        out_specs.append(row)
    out_shape.append(jax.ShapeDtypeStruct((B, T, d), out_dtype))
    out_specs.append(row)
    outs = pl.pallas_call(
        functools.partial(_resnorm_kernel, has_res=res is not None, y_transposed=y_transposed,
                          has_mod=mod is not None),
        grid=(B, nt),
        in_specs=specs,
        out_specs=out_specs,
        out_shape=out_shape,
        compiler_params=_params(("parallel", "parallel"), 48),
        name="resnorm",
    )(*args)
    if res is not None:
        return outs[0], outs[1]
    return None, outs[0]


def _sgu_kernel(pu_ref, pv_ref, ng_ref, w_ref, b_ref, oa_ref, vn_ref, *, groups):
    u = jax.nn.gelu(pu_ref[...])
    v = jax.nn.gelu(pv_ref[...])
    dg = v.shape[1] // groups
    for g in range(groups):
        sl = slice(g * dg, (g + 1) * dg)
        vg = v[:, sl]
        mean = jnp.mean(vg, axis=-1, keepdims=True)
        cen = vg - mean
        var = jnp.mean(cen * cen, axis=-1, keepdims=True)
        vn = cen * lax.rsqrt(var + EPS) * ng_ref[:, sl]
        vn_ref[:, sl] = vn
        s = jnp.dot(w_ref[g], vn.astype(BF16), preferred_element_type=F32) + b_ref[:, sl]
        oa_ref[:, sl] = (u[:, sl] * s).astype(oa_ref.dtype)


def spatial_gating(p, d_a, norm_g, w_s, b_s):
    B, T, _ = p.shape
    L = min(T, MLP_CHUNK)
    G = w_s.shape[0]
    dg = d_a // G
    w_tril = jnp.tril(w_s)[:, :L, :L].astype(BF16)
    b_full = jnp.repeat(b_s.T[:L], dg, axis=1)
    blk = lambda c: pl.BlockSpec((None, L, d_a), lambda b, i: (b, i, c))
    return pl.pallas_call(
        functools.partial(_sgu_kernel, groups=G),
        grid=(B, T // L),
        in_specs=[blk(0), blk(1),
                  pl.BlockSpec((1, d_a), lambda b, i: (0, 0)),
                  pl.BlockSpec((G, L, L), lambda b, i: (0, 0, 0)),
                  pl.BlockSpec((L, d_a), lambda b, i: (0, 0))],
        out_specs=[blk(0), blk(0)],
        out_shape=[jax.ShapeDtypeStruct((B, T, d_a), BF16), jax.ShapeDtypeStruct((B, T, d_a), F32)],
        compiler_params=_params(("parallel", "parallel"), 32),
        name="spatial_gating",
    )(p, p, norm_g.reshape(1, d_a), w_tril, b_full)


def _shifted(x_ref, halo_ref, state_ref, mu_ref):
    x = x_ref[...]
    first = jnp.where(pl.program_id(1) == 0, state_ref[...], halo_ref[7:8, :])
    rows = lax.broadcasted_iota(jnp.int32, x.shape, 0)
    prev = jnp.where(rows == 0, first, pltpu.roll(x, 1, axis=0))
    return x + (prev - x) * mu_ref[...]


def _rwkv_prep_kernel(r_ref, k_ref, v_ref, t_ref, rh_ref, kh_ref, vh_ref, th_ref,
                      sr_ref, sk_ref, sv_ref, st_ref, mr_ref, mk_ref, mv_ref, mt_ref,
                      w0_ref, wup_ref, a0_ref, aup_ref, gup_ref,
                      ro_ref, wo_ref, ko_ref, vo_ref, ao_ref, go_ref, *, lw, la, lg):
    r = _shifted(r_ref, rh_ref, sr_ref, mr_ref)
    k = _shifted(k_ref, kh_ref, sk_ref, mk_ref)
    v = _shifted(v_ref, vh_ref, sv_ref, mv_ref)
    t = _shifted(t_ref, th_ref, st_ref, mt_ref)
    wd, ad, gd = t[:, :lw], t[:, lw:lw + la], t[:, lw + la:lw + la + lg]
    wl = w0_ref[...] + jnp.dot(jnp.tanh(wd).astype(BF16), wup_ref[...], preferred_element_type=F32)
    w_log = -jax.nn.softplus(-wl) - 0.5
    wo_ref[...] = jnp.exp(-jnp.exp(w_log))
    a = jax.nn.sigmoid(a0_ref[...] + jnp.dot(ad.astype(BF16), aup_ref[...], preferred_element_type=F32))
    go_ref[...] = jnp.dot(jax.nn.sigmoid(gd).astype(BF16), gup_ref[...], preferred_element_type=F32)
    ro_ref[...] = r
    vo_ref[...] = v
    ao_ref[...] = a
    ko_ref[...] = k


def rwkv_prep(p, d_b, col0, shift_state, mu, w0, w_up, a0, a_up, g_up):
    B, T, _ = p.shape
    lw, la, lg = w_up.shape[0], a_up.shape[0], g_up.shape[0]
    tt = min(T, 128)
    cb = col0 // d_b
    tb = (col0 + 3 * d_b) // TAIL_BLOCK
    assert col0 % d_b == 0 and (col0 + 3 * d_b) % TAIL_BLOCK == 0 and lw + la + lg <= TAIL_BLOCK
    pad = TAIL_BLOCK - (lw + la + lg)

    def seg(c, width):
        return pl.BlockSpec((None, tt, width), lambda b, i: (b, i, c))

    def halo(c, width):
        return pl.BlockSpec((None, 8, width), lambda b, i: (b, jnp.maximum(i * (tt // 8) - 1, 0), c))

    def per_b(width):
        return pl.BlockSpec((None, 1, width), lambda b, i: (b, 0, 0))

    def full(shape):
        return pl.BlockSpec(shape, lambda b, i: tuple(0 for _ in shape))

    st = shift_state
    s_parts = [st[..., :d_b], st[..., d_b:2 * d_b], st[..., 2 * d_b:3 * d_b],
               jnp.pad(st[..., 3 * d_b:], ((0, 0), (0, 0), (0, pad)))]
    m_parts = [mu[None, :d_b], mu[None, d_b:2 * d_b], mu[None, 2 * d_b:3 * d_b],
               jnp.pad(mu[None, 3 * d_b:], ((0, 0), (0, pad)))]
    out_row = pl.BlockSpec((None, tt, d_b), lambda b, i: (b, i, 0))
    outs = pl.pallas_call(
        functools.partial(_rwkv_prep_kernel, lw=lw, la=la, lg=lg),
        grid=(B, T // tt),
        in_specs=[seg(cb, d_b), seg(cb + 1, d_b), seg(cb + 2, d_b), seg(tb, TAIL_BLOCK),
                  halo(cb, d_b), halo(cb + 1, d_b), halo(cb + 2, d_b), halo(tb, TAIL_BLOCK),
                  per_b(d_b), per_b(d_b), per_b(d_b), per_b(TAIL_BLOCK),
                  full((1, d_b)), full((1, d_b)), full((1, d_b)), full((1, TAIL_BLOCK)),
                  full((1, d_b)), full((lw, d_b)), full((1, d_b)), full((la, d_b)), full((lg, d_b))],
        out_specs=[out_row] * 6,
        out_shape=[jax.ShapeDtypeStruct((B, T, d_b), F32)] * 6,
        compiler_params=_params(("parallel", "arbitrary"), 56),
        name="rwkv_prep",
    )(p, p, p, p, p, p, p, p, *s_parts, *m_parts,
      w0[None], w_up.astype(BF16), a0[None], a_up.astype(BF16), g_up.astype(BF16))
    return outs


def _rwkv_scan_kernel(r_ref, w_ref, k_ref, v_ref, a_ref, kk_ref, ka_ref, rk_ref, lnw_ref, lnb_ref, s0_ref,
                      y_ref, so_ref, s_scr, av_scr, bv_scr, k2_scr, *, tc, dh):
    @pl.when(pl.program_id(1) == 0)
    def _():
        s_scr[...] = s0_ref[...]

    def key_vectors(t):
        kr = k_ref[t] * kk_ref[...]
        kk = kr * lax.rsqrt(jnp.maximum(jnp.sum(kr * kr, axis=0, keepdims=True), 1e-24))
        return -kk, kk * a_ref[t]

    def step(t, sa):
        a = a_ref[t]
        k2 = k_ref[t] * (1.0 + (a - 1.0) * ka_ref[...])
        k2_scr[...] = k2
        av_next, bv_next = key_vectors(jnp.minimum(t + 1, tc - 1))
        av_scr[...] = av_next
        v = v_ref[t]
        y = jnp.zeros_like(v)
        sa_next = jnp.zeros_like(v)
        for k in range(dh):
            s_new = (s_scr[k] * w_ref[t, k:k + 1, :] + sa * bv_scr[k:k + 1, :] + v * k2_scr[k:k + 1, :])
            s_scr[k] = s_new
            y = y + s_new * r_ref[t, k:k + 1, :]
            sa_next = sa_next + s_new * av_scr[k:k + 1, :]
        bv_scr[...] = bv_next
        mean = jnp.mean(y, axis=0, keepdims=True)
        cen = y - mean
        var = jnp.mean(cen * cen, axis=0, keepdims=True)
        yn = cen * lax.rsqrt(var + RWKV_GN_EPS) * lnw_ref[...] + lnb_ref[...]
        bonus = jnp.sum(r_ref[t] * k2 * rk_ref[...], axis=0, keepdims=True) * v
        y_ref[t] = yn + bonus
        return sa_next

    av0, bv0 = key_vectors(0)
    av_scr[...] = av0
    bv_scr[...] = bv0
    sa0 = jnp.zeros((dh, LANES), F32)
    for k in range(dh):
        sa0 = sa0 + s_scr[k] * av_scr[k:k + 1, :]
    lax.fori_loop(0, tc, step, sa0)

    @pl.when(pl.program_id(1) == pl.num_programs(1) - 1)
    def _():
        so_ref[...] = s_scr[...]


def rwkv_scan(r, w, k, v, a, S0, k_k, k_a, r_k, ln_w, ln_b):
    B, T, d_b = r.shape
    H = d_b // DH_B
    BH = B * H
    BHp = _round_up(BH, LANES)

    def to_lanes(x):
        x = x.reshape(B, T, H, DH_B).transpose(1, 3, 0, 2).reshape(T, DH_B, BH)
        return jnp.pad(x, ((0, 0), (0, 0), (0, BHp - BH)))

    def param_lanes(x):
        x = jnp.tile(x.T, (1, B))
        return jnp.pad(x, ((0, 0), (0, BHp - BH)))

    s0 = S0.transpose(3, 2, 0, 1).reshape(DH_B, DH_B, BH)
    s0 = jnp.pad(s0, ((0, 0), (0, 0), (0, BHp - BH)))
    tc = min(T, 64)
    seq = pl.BlockSpec((tc, DH_B, LANES), lambda l, i: (i, 0, l))
    par = pl.BlockSpec((DH_B, LANES), lambda l, i: (0, l))
    st = pl.BlockSpec((DH_B, DH_B, LANES), lambda l, i: (0, 0, l))
    y, s_fin = pl.pallas_call(
        functools.partial(_rwkv_scan_kernel, tc=tc, dh=DH_B),
        grid=(BHp // LANES, T // tc),
        in_specs=[seq] * 5 + [par] * 5 + [st],
        out_specs=[seq, st],
        out_shape=[jax.ShapeDtypeStruct((T, DH_B, BHp), F32), jax.ShapeDtypeStruct((DH_B, DH_B, BHp), F32)],
        scratch_shapes=[pltpu.VMEM((DH_B, DH_B, LANES), F32)] + [pltpu.VMEM((DH_B, LANES), F32)] * 3,
        compiler_params=_params(("parallel", "arbitrary"), 40),
        name="rwkv_scan",
    )(to_lanes(r), to_lanes(w), to_lanes(k), to_lanes(v), to_lanes(a),
      param_lanes(k_k.reshape(H, DH_B)), param_lanes(k_a.reshape(H, DH_B)), param_lanes(r_k),
      param_lanes(ln_w.reshape(H, DH_B)), param_lanes(ln_b.reshape(H, DH_B)), s0)
    y = y[:, :, :BH].reshape(T, DH_B, B, H).transpose(2, 0, 3, 1).reshape(B, T, d_b)
    s_fin = s_fin[:, :, :BH].reshape(DH_B, DH_B, B, H).transpose(2, 3, 1, 0)
    return y, s_fin


def _gate_concat_kernel(oa_ref, yb_ref, g_ref, o_ref, *, d_a):
    o_ref[:, :d_a] = oa_ref[...]
    o_ref[:, d_a:] = (yb_ref[...] * g_ref[...]).astype(o_ref.dtype)


def gate_concat(out_a, yb, g):
    B, T, d_a = out_a.shape
    d_b = yb.shape[-1]
    tt = min(T, 256)
    blk = lambda w: pl.BlockSpec((None, tt, w), lambda b, i: (b, i, 0))
    return pl.pallas_call(
        functools.partial(_gate_concat_kernel, d_a=d_a),
        grid=(B, T // tt),
        in_specs=[blk(d_a), blk(d_b), blk(d_b)],
        out_specs=blk(d_a + d_b),
        out_shape=jax.ShapeDtypeStruct((B, T, d_a + d_b), BF16),
        compiler_params=_params(("parallel", "parallel"), 32),
        name="gate_concat",
    )(out_a, yb, g)


def _rope(x, cos2, sin2):
    return x * cos2 + pltpu.roll(x, x.shape[-1] // 2, axis=1) * sin2


def _retention_kernel(q_ref, k_ref, v_ref, g_ref, cos_ref, sin_ref, dm_ref, kd_ref, qd_ref, gl_ref, gn_ref,
                      r0_ref, o_ref, ro_ref, r_scr, *, L, n_chunks):
    @pl.when(pl.program_id(2) == 0)
    def _():
        r_scr[...] = r0_ref[...]

    scale = DK_C ** -0.5
    for c in range(n_chunks):
        rows = slice(c * L, (c + 1) * L)
        cos2, sin2 = cos_ref[rows, :], sin_ref[rows, :]
        q = _rope(q_ref[rows, :], cos2, sin2)
        k = _rope(k_ref[rows, :], cos2, sin2) * scale
        v = v_ref[rows, :].astype(BF16)
        att = lax.dot_general(q.astype(BF16), k.astype(BF16), (((1,), (1,)), ((), ())),
                              preferred_element_type=F32) * dm_ref[...]
        R = r_scr[...]
        o = jnp.dot(att.astype(BF16), v, preferred_element_type=F32)
        o = o + jnp.dot((q * qd_ref[...]).astype(BF16), R.astype(BF16), preferred_element_type=F32)
        kd_t = (k * kd_ref[...]).T.astype(BF16)
        r_scr[...] = R * gl_ref[...] + jnp.dot(kd_t, v, preferred_element_type=F32)
        mean = jnp.mean(o, axis=-1, keepdims=True)
        cen = o - mean
        var = jnp.mean(cen * cen, axis=-1, keepdims=True)
        on = cen * lax.rsqrt(var + EPS) * gn_ref[...]
        gate = g_ref[rows, :]
        o_ref[rows, :] = (gate * jax.nn.sigmoid(gate) * on).astype(o_ref.dtype)

    @pl.when(pl.program_id(2) == pl.num_programs(2) - 1)
    def _():
        ro_ref[...] = r_scr[...]


def retention(p, h_c, pos, R0, gn_g):
    B, T, _ = p.shape
    L = math.gcd(T, RET_CHUNK)
    tr = min(T, 512)
    n_chunks = tr // L
    half = DK_C // 2
    inv = ROPE_BASE ** (-jnp.arange(half, dtype=F32) / half)
    ang = pos.astype(F32)[:, None] * inv[None, :]
    cos2 = jnp.concatenate([jnp.cos(ang), jnp.cos(ang)], axis=-1)
    sin2 = jnp.concatenate([-jnp.sin(ang), jnp.sin(ang)], axis=-1)
    log_g = jnp.log1p(-jnp.exp2(-5.0 - jnp.arange(h_c, dtype=F32)))
    idx = jnp.arange(L, dtype=F32)
    rel = idx[:, None] - idx[None, :]
    dmat = jnp.where(rel >= 0, jnp.exp(jnp.maximum(rel, 0.0) * log_g[:, None, None]), 0.0)
    k_dec = jnp.exp((L - 1 - idx)[None, :, None] * log_g[:, None, None])
    q_dec = jnp.exp((idx + 1)[None, :, None] * log_g[:, None, None])
    k_dec = jnp.broadcast_to(k_dec, (h_c, L, DK_C))
    q_dec = jnp.broadcast_to(q_dec, (h_c, L, DK_C))
    g_l = jnp.broadcast_to(jnp.exp(L * log_g)[:, None, None], (h_c, 1, DV_C))

    def col(width, c0):
        return pl.BlockSpec((None, tr, width), lambda b, h, i: (b, i, c0 + h))

    def head(shape):
        return pl.BlockSpec((None,) + shape, lambda b, h, i: (h,) + tuple(0 for _ in shape))

    trig = pl.BlockSpec((tr, DK_C), lambda b, h, i: (i, 0))
    state = pl.BlockSpec((None, None, DK_C, DV_C), lambda b, h, i: (b, h, 0, 0))
    return pl.pallas_call(
        functools.partial(_retention_kernel, L=L, n_chunks=n_chunks),
        grid=(B, h_c, T // tr),
        in_specs=[col(DK_C, 0), col(DK_C, h_c), col(DV_C, h_c), col(DV_C, 2 * h_c), trig, trig,
                  head((L, L)), head((L, DK_C)), head((L, DK_C)), head((1, DV_C)),
                  pl.BlockSpec((1, DV_C), lambda b, h, i: (0, h)), state],
        out_specs=[pl.BlockSpec((None, tr, DV_C), lambda b, h, i: (b, i, h)), state],
        out_shape=[jax.ShapeDtypeStruct((B, T, h_c * DV_C), BF16),
                   jax.ShapeDtypeStruct((B, h_c, DK_C, DV_C), F32)],
        scratch_shapes=[pltpu.VMEM((DK_C, DV_C), F32)],
        compiler_params=_params(("parallel", "parallel", "arbitrary"), 32),
        name="retention",
    )(p, p, p, p, cos2, sin2, dmat, k_dec, q_dec, g_l, gn_g.reshape(1, h_c * DV_C), R0)


def _fox_kernel(q_ref, k_ref, v_ref, fq_ref, fk_ref, o_ref, kb_scr, vt_scr, acc_scr, *, tq, tkc, past):
    qi = pl.program_id(2)
    n_kv = k_ref.shape[0] // tkc

    @pl.when(qi == 0)
    def _():
        for c in range(n_kv):
            rows = slice(c * tkc, (c + 1) * tkc)
            kb_scr[rows, :] = k_ref[rows, :].astype(BF16)
            vt_scr[:, rows] = v_ref[rows, :].T.astype(BF16)

    q = (q_ref[...] * (DH_D ** -0.5 * LOG2E)).astype(BF16)
    fq = fq_ref[...]
    n_rep = max(tq // LANES, 1)

    def scores(c, masked):
        r0 = 0 if n_kv == 1 else pl.multiple_of(c * tkc, tkc)
        s = lax.dot_general(kb_scr[pl.ds(r0, tkc), :], q, (((1,), (1,)), ((), ())), preferred_element_type=F32)
        fk = fk_ref[pl.ds(r0, tkc), :]
        fk = jnp.concatenate([fk] * n_rep, axis=1) if tq >= LANES else fk[:, :tq]
        s = s + (fq - fk)
        if masked:
            kpos = r0 + lax.broadcasted_iota(jnp.int32, s.shape, 0)
            qpos = past + qi * tq + lax.broadcasted_iota(jnp.int32, s.shape, 1)
            s = jnp.where(kpos <= qpos, s, -jnp.inf)
        return r0, s

    def chunks(cs, carry, masked):
        m_old, l_old = carry
        parts = [scores(c, masked) for c in cs]
        m_new = m_old
        for _, s in parts:
            m_new = jnp.maximum(m_new, jnp.max(s, axis=0, keepdims=True))
        alpha = jnp.exp2(m_old - m_new)
        l_new = alpha * l_old
        upd = alpha * acc_scr[...]
        for r0, s in parts:
            p = jnp.exp2(s - m_new)
            l_new = l_new + jnp.sum(p, axis=0, keepdims=True)
            upd = upd + jnp.dot(vt_scr[:, pl.ds(r0, tkc)], p.astype(BF16), preferred_element_type=F32)
        acc_scr[...] = upd
        return m_new, l_new

    def sweep(lo, hi, carry, masked):
        for width in (4, 2, 1):
            n_grp = (hi - lo) // width
            carry = lax.fori_loop(
                0, n_grp, lambda i, cr, lo=lo, width=width: chunks(tuple(lo + width * i + u for u in range(width)),
                                                                    cr, masked), carry)
            lo = lo + width * n_grp
        return carry

    acc_scr[...] = jnp.zeros_like(acc_scr)
    carry = (jnp.full((1, tq), -jnp.inf, F32), jnp.zeros((1, tq), F32))
    n_full = (past + qi * tq + 1) // tkc
    n_need = (past + qi * tq + tq - 1) // tkc + 1
    carry = sweep(0, n_full, carry, False)
    m_fin, l_fin = sweep(n_full, n_need, carry, True)
    o_ref[...] = (acc_scr[...] / l_fin).T.astype(o_ref.dtype)


def fox_attention(q_src, q_col, k_src, k_col, v_src, v_col, F_all, n_heads, T):
    B = q_src.shape[0]
    Tk = k_src.shape[1]
    past = Tk - T
    tq = min(T, 1024)
    tkc = 256 if Tk % 256 == 0 else Tk
    assert T % tq == 0
    f_heads = F_all.transpose(0, 2, 1) * LOG2E
    f_row = f_heads[:, :, None, past:]
    f_rep = jnp.broadcast_to(f_heads[..., None], (B, n_heads, Tk, LANES))

    def kv(col0):
        return pl.BlockSpec((None, Tk, DH_D), lambda b, h, i: (b, 0, col0 + h))

    return pl.pallas_call(
        functools.partial(_fox_kernel, tq=tq, tkc=tkc, past=past),
        grid=(B, n_heads, T // tq),
        in_specs=[pl.BlockSpec((None, tq, DH_D), lambda b, h, i: (b, i, q_col + h)),
                  kv(k_col), kv(v_col),
                  pl.BlockSpec((None, None, 1, tq), lambda b, h, i: (b, h, 0, i)),
                  pl.BlockSpec((None, None, Tk, LANES), lambda b, h, i: (b, h, 0, 0))],
        out_specs=pl.BlockSpec((None, tq, DH_D), lambda b, h, i: (b, i, h)),
        out_shape=jax.ShapeDtypeStruct((B, T, n_heads * DH_D), BF16),
        scratch_shapes=[pltpu.VMEM((Tk, DH_D), BF16), pltpu.VMEM((DH_D, Tk), BF16), pltpu.VMEM((DH_D, tq), F32)],
        compiler_params=_params(("parallel", "parallel", "arbitrary"), 40),
        name="fox_attention",
    )(q_src, k_src, v_src, f_row, f_rep)


SUBLANES = 8


def _bitonic_merge_desc(v):
    v = list(v)
    n = len(v)
    step = n // 2
    while step >= 1:
        for i in range(n):
            if i & step == 0:
                a, b = v[i], v[i + step]
                v[i], v[i + step] = jnp.maximum(a, b), jnp.minimum(a, b)
        step //= 2
    return v


def _bitonic_sort_desc(v):
    v = list(v)
    n = len(v)
    size = 2
    while size <= n:
        step = size // 2
        while step >= 1:
            for i in range(n):
                if i & step == 0:
                    a, b = v[i], v[i + step]
                    hi, lo = jnp.maximum(a, b), jnp.minimum(a, b)
                    v[i], v[i + step] = (hi, lo) if i & size == 0 else (lo, hi)
            step //= 2
        size *= 2
    return v


def _merge_top(a, b):
    n = len(a)
    c = [jnp.maximum(a[i], b[n - 1 - i]) if n - 1 - i < len(b) else a[i] for i in range(n)]
    return _bitonic_merge_desc(c)


def _top_of_keys(s):
    slots = [s[SUBLANES * j:SUBLANES * (j + 1), :] for j in range(s.shape[0] // SUBLANES)]
    assert len(slots) == TOPK_P
    v = _bitonic_sort_desc(slots)
    shift = SUBLANES // 2
    while shift >= 1:
        partner = [pltpu.roll(x, shift, axis=0) for x in v]
        v = _merge_top(v, partner)
        shift //= 2
    return v


def _peer_score_kernel(h_ref, wq_ref, sk_ref, s1_ref, s2_ref, tau_ref, mz_ref, *, n_heads, n_keys):
    qT = lax.dot_general(wq_ref[...], h_ref[...], (((1,), (1,)), ((), ())), preferred_element_type=F32)
    dq = qT.shape[0] // (2 * n_heads)
    t = qT.shape[1]
    head_row = lax.broadcasted_iota(jnp.int32, (SUBLANES, t), 0)
    tops = [[jnp.zeros((SUBLANES, t), F32)] * TOPK_P for _ in range(2)]
    for hd in range(n_heads):
        for c in range(2):
            hc = 2 * hd + c
            q = qT[hc * dq:(hc + 1) * dq, :].astype(BF16)
            s = jnp.dot(sk_ref[hc], q, preferred_element_type=F32)
            if c == 0:
                s1_ref[:, hd, :] = s
            else:
                s2_ref[hd] = s
            top = _top_of_keys(s)
            tops[c] = [jnp.where(head_row == hd, top[i], tops[c][i]) for i in range(TOPK_P)]
    v1, v2 = tops
    best = [v1[0] + v2[b] for b in range(TOPK_P)]
    for a in range(1, TOPK_P):
        best = _merge_top(best, [v1[a] + v2[b] for b in range(TOPK_P // (a + 1))])
    z = jnp.zeros_like(best[0])
    for s_k in best:
        z = z + jnp.exp(s_k - best[0])
    tau_ref[...] = best[TOPK_P - 1]
    mz_ref[...] = best[0] + jnp.log(z)


def peer_scores(h2d, wq_t, subkeys_b):
    N, d = h2d.shape
    n_hc, n_keys, dq = subkeys_b.shape
    n_heads = n_hc // 2
    assert n_heads == SUBLANES and n_keys == TOPK_P * SUBLANES
    tt = min(N, 256)
    return pl.pallas_call(
        functools.partial(_peer_score_kernel, n_heads=n_heads, n_keys=n_keys),
        grid=(N // tt,),
        in_specs=[pl.BlockSpec((tt, d), lambda i: (i, 0)),
                  pl.BlockSpec(wq_t.shape, lambda i: (0, 0), pipeline_mode=pl.Buffered(1)),
                  pl.BlockSpec(subkeys_b.shape, lambda i: (0, 0, 0))],
        out_specs=[pl.BlockSpec((n_keys, n_heads, tt), lambda i: (0, 0, i)),
                   pl.BlockSpec((n_heads, n_keys, tt), lambda i: (0, 0, i)),
                   pl.BlockSpec((n_heads, tt), lambda i: (0, i)),
                   pl.BlockSpec((n_heads, tt), lambda i: (0, i))],
        out_shape=[jax.ShapeDtypeStruct((n_keys, n_heads, N), F32),
                   jax.ShapeDtypeStruct((n_heads, n_keys, N), F32),
                   jax.ShapeDtypeStruct((n_heads, N), F32),
                   jax.ShapeDtypeStruct((n_heads, N), F32)],
        compiler_params=_params(("parallel",), 48),
        name="peer_scores",
    )(h2d, wq_t, subkeys_b)


def _peer_expert_kernel(h_ref, u_ref, vt_ref, s1_ref, s2_ref, tau_ref, mz_ref, o_ref, w0_scr, w1_scr, g0_scr, g1_scr,
                        *, n_heads, n_keys, rows, n_tiles):
    j = pl.program_id(1)
    tt = h_ref.shape[0]
    d = o_ref.shape[0]
    cw = min(tt, LANES)
    w_scr = (w0_scr, w1_scr)
    g_scr = (g0_scr, g1_scr)
    gate_pieces = [(r, c0) for r in range(rows) for c0 in range(0, tt, cw)]
    n_g = len(gate_pieces)

    def gate_piece(tile, g_ref, r, c0):
        cols = slice(c0, c0 + cw)
        s1_heads = s1_ref[tile * rows + r]
        g = jnp.zeros((n_keys, cw), F32)
        for hd in range(n_heads):
            s = s1_heads[hd:hd + 1, cols] + s2_ref[hd, :, cols]
            e = jnp.exp(s - mz_ref[hd:hd + 1, cols])
            g = g + jnp.where(s >= tau_ref[hd:hd + 1, cols], e, 0.0)
        g_ref[r * n_keys:(r + 1) * n_keys, cols] = g

    def score_piece(g_ref, w_ref, piece, n_p):
        cols = slice(piece * (tt // n_p), (piece + 1) * (tt // n_p))
        hT = lax.dot_general(u_ref[...], h_ref[cols, :], (((1,), (1,)), ((), ())), preferred_element_type=F32)
        w_ref[:, cols] = (jax.nn.gelu(hT) * g_ref[:, cols]).astype(BF16)

    def accumulate_piece(w_ref, m, n_m):
        rws = slice(m * (d // n_m), (m + 1) * (d // n_m))
        o_ref[rws, :] += jnp.dot(vt_ref[rws, :], w_ref[...], preferred_element_type=F32)

    n_sp = 2 if tt >= 2 * LANES else 1

    @pl.when(j == 0)
    def _():
        o_ref[...] = jnp.zeros_like(o_ref)
        for r, c0 in gate_pieces:
            gate_piece(j, g_scr[0], r, c0)
        for p in range(n_sp):
            score_piece(g_scr[0], w_scr[0], p, n_sp)
        for r, c0 in gate_pieces:
            gate_piece(j + 1, g_scr[1], r, c0)

    for par in range(2):
        @pl.when(jnp.logical_and(jnp.logical_and(j > 0, j < n_tiles), j % 2 == par))
        def _():
            nxt = jnp.minimum(j + 1, n_tiles - 1)
            for m in range(n_g):
                accumulate_piece(w_scr[1 - par], m, n_g)
                gate_piece(nxt, g_scr[1 - par], *gate_pieces[m])
            for p in range(n_sp):
                score_piece(g_scr[par], w_scr[par], p, n_sp)

    @pl.when(j == n_tiles)
    def _():
        accumulate_piece(w_scr[(n_tiles - 1) % 2], 0, 1)


def peer_experts(h2d, u_all, vt_all, layer, s1, s2, tau, mz):
    N, d = h2d.shape
    n_exp = u_all.shape[1]
    n_heads, n_keys, _ = s2.shape
    tt = min(N, 512)
    te = 512
    rows = te // n_keys
    n_tiles = n_exp // te
    assert n_tiles % 2 == 0
    once = pl.Buffered(1)
    return pl.pallas_call(
        functools.partial(_peer_expert_kernel, n_heads=n_heads, n_keys=n_keys, rows=rows, n_tiles=n_tiles),
        grid=(N // tt, n_tiles + 1),
        in_specs=[pl.BlockSpec((tt, d), lambda i, j: (i, 0), pipeline_mode=once),
                  pl.BlockSpec((None, te, d), lambda i, j: (layer, jnp.minimum(j, n_tiles - 1), 0)),
                  pl.BlockSpec((None, d, te), lambda i, j: (layer, 0, jnp.maximum(j - 1, 0))),
                  pl.BlockSpec((n_keys, n_heads, tt), lambda i, j: (0, 0, i), pipeline_mode=once),
                  pl.BlockSpec((n_heads, n_keys, tt), lambda i, j: (0, 0, i), pipeline_mode=once),
                  pl.BlockSpec((n_heads, tt), lambda i, j: (0, i), pipeline_mode=once),
                  pl.BlockSpec((n_heads, tt), lambda i, j: (0, i), pipeline_mode=once)],
        out_specs=pl.BlockSpec((d, tt), lambda i, j: (0, i)),
        out_shape=jax.ShapeDtypeStruct((d, N), F32),
        scratch_shapes=[pltpu.VMEM((te, tt), BF16)] * 2 + [pltpu.VMEM((te, tt), F32)] * 2,
        compiler_params=_params(("parallel", "arbitrary"), 56),
        name="peer_experts",
    )(h2d, u_all, vt_all, s1, s2, tau, mz)


def peer_ffn(h, wq_t, subkeys_b, u_all, vt_all, layer):
    B, T, d = h.shape
    h2d = h.reshape(B * T, d)
    s1, s2, tau, mz = peer_scores(h2d, wq_t, subkeys_b)
    return peer_experts(h2d, u_all, vt_all, layer, s1, s2, tau, mz)


def _pad_cols(w, n):
    return jnp.pad(w, ((0, 0), (0, n - w.shape[1])))


def kernel(x_prompt, x_sample, c_prompt, c_sample, state_rwkv_shift, state_rwkv, state_ret, cache_fox_k, cache_fox_v, cache_fox_logf, ada_w, ada_b, norm_g, final_g, w_in_even, w_out_even, sgu_norm_g, sgu_w, sgu_b, rwkv_mu, rwkv_w0, rwkv_w_up, rwkv_a0, rwkv_a_up, rwkv_g_up, rwkv_k_k, rwkv_k_a, rwkv_r_k, rwkv_ln_w, rwkv_ln_b, w_in_odd, w_out_odd, ret_gn_g, fox_b_f, peer_wq, peer_subkeys, peer_u, peer_v):
    depth = ada_w.shape[0]
    d = x_prompt.shape[-1]
    d_a = d_b = d // 2
    h_b = d_b // DH_B
    h_c = d // (2 * DV_C)
    h_d = d // (2 * DH_D)
    p_b = rwkv_mu.shape[1]
    n_bp, n_bs = x_prompt.shape[0], x_sample.shape[0]
    past_len = cache_fox_k.shape[2]

    n_rows = _round_up(n_bp + n_bs, 16)
    c_all = jnp.pad(jnp.concatenate([c_prompt, c_sample], axis=0), ((0, n_rows - n_bp - n_bs), (0, 0)))
    mod4 = ada_modulation(c_all, ada_w, ada_b).reshape(depth, n_rows, 1, 6 * d)

    w_in_e = [_pad_cols(w_in_even[j], 2 * d_a + 3 * d_b + TAIL_BLOCK).astype(BF16) for j in range(w_in_even.shape[0])]
    w_in_o = [_pad_cols(w_in_odd[j], _round_up(w_in_odd.shape[2], 512)).astype(BF16) for j in range(w_in_odd.shape[0])]
    w_out_e = [w_out_even[j].astype(BF16) for j in range(w_out_even.shape[0])]
    w_out_o = [w_out_odd[j].astype(BF16) for j in range(w_out_odd.shape[0])]
    n_hp = peer_subkeys.shape[1]
    wq_t = [peer_wq[i].T.astype(BF16) for i in range(depth)]
    subk = [peer_subkeys[i].reshape(2 * n_hp, peer_subkeys.shape[3], peer_subkeys.shape[4]).astype(BF16)
            for i in range(depth)]
    u_all = peer_u.astype(BF16)
    vt_all = jnp.swapaxes(peer_v, 1, 2).astype(BF16)

    def run(x, row0, pos0, shift0, rwkv0, ret0, fox_past):
        B, T, _ = x.shape
        pos = pos0 + jnp.arange(T)
        v_rows, shifts, rwkvs, rets, ks, vs, lfs = [], [], [], [], [], [], []
        pending = None
        for i in range(depth):
            j = i // 2
            res = None if pending is None else (pending[0], pending[1], mod4, pending[3], row0, pending[2])
            x_new, h = resnorm(x, norm_g[i, 0], res=res, mod=(mod4, i, row0, 1, 0))
            x = x if x_new is None else x_new
            h2d = h.reshape(B * T, d)
            if i % 2 == 0:
                p = matmul(h2d, w_in_e[j]).reshape(B, T, -1)
                out_a, vn = spatial_gating(p, d_a, sgu_norm_g[j], sgu_w[j], sgu_b[j])
                r, w, k, v, a, g = rwkv_prep(p, d_b, 2 * d_a, shift0[j], rwkv_mu[j], rwkv_w0[j], rwkv_w_up[j],
                                             rwkv_a0[j], rwkv_a_up[j], rwkv_g_up[j])
                yield "relayout"
                yb, S = rwkv_scan(r, w, k, v, a, rwkv0[j], rwkv_k_k[j], rwkv_k_a[j], rwkv_r_k[j], rwkv_ln_w[j],
                                  rwkv_ln_b[j])
                mix = gate_concat(out_a, yb, g)
                y = matmul(mix.reshape(B * T, d), w_out_e[j]).reshape(B, T, d)
                v_rows.append(vn)
                shifts.append(p[:, -1:, 2 * d_a:2 * d_a + p_b])
                rwkvs.append(S)
            else:
                p = matmul(h2d, w_in_o[j]).reshape(B, T, -1)
                c_dq = 2 * h_c * DK_C + 2 * h_c * DV_C
                c_dk, c_dv, c_df = c_dq + h_d * DH_D, c_dq + 2 * h_d * DH_D, c_dq + 3 * h_d * DH_D
                out_c, R = retention(p, h_c, pos, ret0[j], ret_gn_g[j])
                logf = jax.nn.log_sigmoid(p[..., c_df:c_df + h_d] + fox_b_f[j])
                kd = p[..., c_dk:c_dk + h_d * DH_D]
                vd = p[..., c_dv:c_dv + h_d * DH_D]
                if fox_past is None:
                    F_all = jnp.cumsum(logf, axis=1)
                    od = fox_attention(p, c_dq // DH_D, p, c_dk // DH_D, p, c_dv // DH_D, F_all, h_d, T)
                else:
                    k_past, v_past, lf_past = fox_past[j]
                    k_all = jnp.concatenate([k_past.reshape(B, past_len, -1), kd], axis=1)
                    v_all = jnp.concatenate([v_past.reshape(B, past_len, -1), vd], axis=1)
                    F_all = jnp.cumsum(jnp.concatenate([lf_past, logf], axis=1), axis=1)
                    od = fox_attention(p, c_dq // DH_D, k_all, 0, v_all, 0, F_all, h_d, T)
                mix = jnp.concatenate([out_c, od], axis=-1)
                y = matmul(mix.reshape(B * T, d), w_out_o[j]).reshape(B, T, d)
                rets.append(R)
                ks.append(kd.reshape(B, T, h_d, DH_D))
                vs.append(vd.reshape(B, T, h_d, DH_D))
                lfs.append(logf)
            x, h = resnorm(x, norm_g[i, 1], res=(y, False, mod4, i, row0, 2), mod=(mod4, i, row0, 4, 3))
            y_t = peer_ffn(h, wq_t[i], subk[i], u_all, vt_all, i)
            if T % LANES == 0:
                pending = (y_t, True, 5, i)
            else:
                pending = (y_t.T.reshape(B, T, d), False, 5, i)
            yield "layer"
        _, out = resnorm(x, final_g, res=(pending[0], pending[1], mod4, pending[3], row0, pending[2]),
                         out_dtype=F32)
        return out, v_rows, shifts, rwkvs, rets, ks, vs, lfs

    n_even, n_odd = w_in_even.shape[0], w_in_odd.shape[0]
    run_p = run(
        x_prompt, 0, 0,
        [jnp.zeros((n_bp, 1, p_b), F32)] * n_even,
        [jnp.zeros((n_bp, h_b, DH_B, DH_B), F32)] * n_even,
        [jnp.zeros((n_bp, h_c, DK_C, DV_C), F32)] * n_odd,
        None)
    run_s = run(
        x_sample, n_bp, past_len,
        [state_rwkv_shift[j] for j in range(n_even)],
        [state_rwkv[j] for j in range(n_even)],
        [state_ret[j] for j in range(n_odd)],
        [(cache_fox_k[j], cache_fox_v[j], cache_fox_logf[j]) for j in range(n_odd)])

    def finish_layer(gen):
        while next(gen) != "layer":
            pass

    def result(gen):
        try:
            next(gen)
        except StopIteration as stop:
            return stop.value
        raise AssertionError("run() yielded after its last layer")

    for _ in range(depth):
        prompt_mid_layer = next(run_p) == "relayout"
        finish_layer(run_s)
        if prompt_mid_layer:
            finish_layer(run_p)
    y_p, _, sh_p, rw_p, rt_p, k_p, v_p, lf_p = result(run_p)
    y_s, vr_s, sh_s, rw_s, rt_s, k_s, v_s, lf_s = result(run_s)
    return (y_p, y_s, jnp.stack(vr_s), jnp.stack(sh_p), jnp.stack(sh_s), jnp.stack(rw_p), jnp.stack(rw_s),
            jnp.stack(rt_p), jnp.stack(rt_s), jnp.stack(k_p), jnp.stack(k_s), jnp.stack(v_p), jnp.stack(v_s),
            jnp.stack(lf_p), jnp.stack(lf_s))
```
